```python
import jax, jax.numpy as jnp
from jax import lax
import numpy as np


D_MODEL = 1024
BATCH = 8
SEQ = 2048
DEPTH = 2
DEC_BATCH = 128
DEC_SEQ = 8
PAST_LEN = 16384
PAGE_SIZE = 128

GLA_HEADS = 4
GLA_DK = 64
GLA_DV = 128
GLA_LOWRANK = 16
GLA_GATE_NORMALIZER = 16.0
GLA_CHUNK = 16
GDN_HEADS = 4
GDN_DK = 128
GDN_DV = 128
GDN_CHUNK = 64
CONV_W = 4
GDN_CONV_DIM = GDN_HEADS * (2 * GDN_DK + GDN_DV)
MIX_WIDTH = GLA_HEADS * GLA_DV + GDN_HEADS * GDN_DV
D_FF_DENSE = 2816
N_EXPERTS = 8
TOP_K = 2
D_FF_EXPERT = 3584
N_DENSE = (DEPTH + 1) // 2
N_MOE = DEPTH // 2
EPS = 1e-6

_SIZES = (GLA_HEADS * GLA_DK, GLA_HEADS * GLA_DK, GLA_HEADS * GLA_DV, GLA_LOWRANK, GLA_HEADS * GLA_DV,
          GDN_CONV_DIM, GDN_HEADS, GDN_HEADS, GDN_HEADS * GDN_DV)
PROJ_DIM = int(sum(_SIZES))
SPLIT_IDX = tuple(int(s) for s in np.cumsum(_SIZES)[:-1])

kernel_name = 'hymba_gla_gdn_adaln_decoder_step'


def rmsnorm(x, w):
    xf = x.astype(jnp.float32)
    y = xf * lax.rsqrt(jnp.mean(xf * xf, axis=-1, keepdims=True) + EPS)
    return (y * w.astype(jnp.float32)).astype(x.dtype)


def l2norm(x):
    xf = x.astype(jnp.float32)
    return xf * lax.rsqrt(jnp.sum(xf * xf, axis=-1, keepdims=True) + EPS)


def gla_chunked(q, k, v, log_a, s0):
    f32 = jnp.float32
    bsz, t_len, nh, _ = q.shape
    dv = v.shape[-1]
    cl = GLA_CHUNK
    n = -(-t_len // cl)
    pw = ((0, 0), (0, n * cl - t_len), (0, 0), (0, 0))
    q, k, v, la = [jnp.pad(a.astype(f32), pw).reshape(bsz, n, cl, nh, a.shape[-1]) for a in (q, k, v, log_a)]
    b = jnp.cumsum(la, axis=2)
    causal = jnp.tril(jnp.ones((cl, cl), dtype=bool))[None, None, :, :, None, None]
    decay = jnp.exp(jnp.where(causal, b[:, :, :, None] - b[:, :, None, :], -jnp.inf))
    attn = jnp.einsum('bntshk,bnthk,bnshk->bnhts', decay, q, k)
    o_intra = jnp.einsum('bnhts,bnshv->bnthv', attn, v)
    b_last = b[:, :, -1]
    q_dec = q * jnp.exp(b)
    k_dec = k * jnp.exp(b_last[:, :, None] - b)

    def step(s, inp):
        qd, kd, vv, bl = inp
        o = jnp.einsum('bthk,bhkv->bthv', qd, s)
        s = s * jnp.exp(bl)[..., None] + jnp.einsum('bthk,bthv->bhkv', kd, vv)
        return s, o

    xs = tuple(jnp.moveaxis(a, 1, 0) for a in (q_dec, k_dec, v, b_last))
    s_fin, o_inter = lax.scan(step, s0.astype(f32), xs)
    o = (o_intra + jnp.moveaxis(o_inter, 0, 1)).reshape(bsz, n * cl, nh, dv)[:, :t_len]
    return o, s_fin.astype(s0.dtype)


def gdn_chunked(q, k, v, g, beta, s0):
    f32 = jnp.float32
    bsz, t_len, nh, _ = q.shape
    dv = v.shape[-1]
    cl = GDN_CHUNK
    n = -(-t_len // cl)
    pad = n * cl - t_len
    q, k, v = [jnp.pad(a.astype(f32), ((0, 0), (0, pad), (0, 0), (0, 0))).reshape(bsz, n, cl, nh, a.shape[-1])
               for a in (q, k, v)]
    g, beta = [jnp.pad(a.astype(f32), ((0, 0), (0, pad), (0, 0))).reshape(bsz, n, cl, nh) for a in (g, beta)]
    gam = jnp.cumsum(g, axis=2)
    gam_h = jnp.moveaxis(gam, 3, 2)
    incl = jnp.tril(jnp.ones((cl, cl), dtype=bool))
    strict = jnp.tril(jnp.ones((cl, cl), dtype=bool), -1)
    lmask = jnp.exp(jnp.where(incl, gam_h[..., :, None] - gam_h[..., None, :], -jnp.inf))
    beta_h = jnp.moveaxis(beta, 3, 2)
    kk = jnp.einsum('bnthk,bnshk->bnhts', k, k)
    m = jnp.where(strict, beta_h[..., :, None] * kk * lmask, 0.0)
    eye = jnp.eye(cl, dtype=f32)
    t_inv = lax.linalg.triangular_solve(eye + m, jnp.broadcast_to(eye, m.shape),
                                        left_side=True, lower=True, unit_diagonal=True)
    u = jnp.einsum('bnhts,bnshv->bnthv', t_inv, v * beta[..., None])
    w = jnp.einsum('bnhts,bnshk->bnthk', t_inv, k * (beta * jnp.exp(gam))[..., None])
    a_qk = jnp.einsum('bnthk,bnshk->bnhts', q, k) * lmask
    q_dec = q * jnp.exp(gam)[..., None]
    g_last = gam[:, :, -1]
    k_dec = k * jnp.exp(g_last[:, :, None] - gam)[..., None]

    def step(s, inp):
        uu, ww, aa, qd, kd, gl = inp
        v_new = uu - jnp.einsum('bthk,bhkv->bthv', ww, s)
        o = jnp.einsum('bthk,bhkv->bthv', qd, s) + jnp.einsum('bhts,bshv->bthv', aa, v_new)
        s = s * jnp.exp(gl)[..., None, None] + jnp.einsum('bthk,bthv->bhkv', kd, v_new)
        return s, o

    xs = tuple(jnp.moveaxis(a, 1, 0) for a in (u, w, a_qk, q_dec, k_dec, g_last))
    s_fin, o = lax.scan(step, s0.astype(f32), xs)
    o = jnp.moveaxis(o, 0, 1).reshape(bsz, n * cl, nh, dv)[:, :t_len]
    return o, s_fin.astype(s0.dtype)


def mixer(h, s_gla, s_gdn, s_conv, w_in, w_gla_gate, b_gla_gate, gla_norm, w_conv, a_log, dt_bias, gdn_norm, w_out):
    f32 = jnp.float32
    bsz, t_len, _ = h.shape
    proj = h @ w_in
    q1, k1, v1, lr, g1, qkv, a_raw, b_raw, z = jnp.split(proj, SPLIT_IDX, axis=-1)
    q1 = q1.reshape(bsz, t_len, GLA_HEADS, GLA_DK) * (GLA_DK ** -0.5)
    k1 = k1.reshape(bsz, t_len, GLA_HEADS, GLA_DK)
    v1 = v1.reshape(bsz, t_len, GLA_HEADS, GLA_DV)
    log_a = jax.nn.log_sigmoid((lr @ w_gla_gate + b_gla_gate).astype(f32)) / GLA_GATE_NORMALIZER
    o1, s_gla_new = gla_chunked(q1, k1, v1, log_a.reshape(bsz, t_len, GLA_HEADS, GLA_DK), s_gla)
    o1 = rmsnorm(o1, gla_norm) * jax.nn.silu(g1.reshape(bsz, t_len, GLA_HEADS, GLA_DV).astype(f32))
    xc = jnp.concatenate([s_conv.astype(qkv.dtype), qkv], axis=1)
    conv = xc[:, 0:t_len] * w_conv[0]
    for j in range(1, CONV_W):
        conv = conv + xc[:, j:j + t_len] * w_conv[j]
    conv = jax.nn.silu(conv)
    s_conv_new = xc[:, t_len:].astype(s_conv.dtype)
    qg, kg, vg = jnp.split(conv, [GDN_HEADS * GDN_DK, 2 * GDN_HEADS * GDN_DK], axis=-1)
    qg = l2norm(qg.reshape(bsz, t_len, GDN_HEADS, GDN_DK)) * (GDN_DK ** -0.5)
    kg = l2norm(kg.reshape(bsz, t_len, GDN_HEADS, GDN_DK))
    vg = vg.reshape(bsz, t_len, GDN_HEADS, GDN_DV)
    g = -jnp.exp(a_log.astype(f32)) * jax.nn.softplus(a_raw.astype(f32) + dt_bias.astype(f32))
    beta = jax.nn.sigmoid(b_raw.astype(f32))
    o2, s_gdn_new = gdn_chunked(qg, kg, vg, g, beta, s_gdn)
    o2 = rmsnorm(o2, gdn_norm) * jax.nn.silu(z.reshape(bsz, t_len, GDN_HEADS, GDN_DV).astype(f32))
    o = jnp.concatenate([o1.reshape(bsz, t_len, -1), o2.reshape(bsz, t_len, -1)], axis=-1).astype(h.dtype)
    return o @ w_out, s_gla_new, s_gdn_new, s_conv_new


def swiglu(h, w_gate, w_up, w_down):
    return (jax.nn.silu(h @ w_gate) * (h @ w_up)) @ w_down


def moe(h, w_router, w_gate, w_up, w_down):
    logits = (h @ w_router).astype(jnp.float32)
    top_v, top_i = lax.top_k(logits, TOP_K)
    top_w = jax.nn.softmax(top_v, axis=-1)
    combine = jnp.sum(jax.nn.one_hot(top_i, N_EXPERTS, dtype=jnp.float32) * top_w[..., None], axis=-2)
    combine = combine.astype(h.dtype)
    y = jnp.zeros_like(h)
    for e in range(N_EXPERTS):
        y = y + combine[..., e:e + 1] * swiglu(h, w_gate[e], w_up[e], w_down[e])
    return y


def setup_inputs(seed: int = 0) -> dict:
    key = jax.random.key(seed)
    ks = jax.random.split(key, 40)
    nrm = lambda i, shape, s: jax.random.normal(ks[i], shape, jnp.float32) * s
    dt = jnp.exp(jax.random.uniform(ks[12], (DEPTH, GDN_HEADS)) * (jnp.log(0.1) - jnp.log(0.001)) + jnp.log(0.001))
    return {
        'x_prompt': nrm(0, (BATCH, SEQ, D_MODEL), 1.0),
        'x_sample': nrm(1, (DEC_BATCH, DEC_SEQ, D_MODEL), 1.0),
        'c_prompt': nrm(2, (BATCH, D_MODEL), 1.0),
        'c_sample': nrm(3, (DEC_BATCH, D_MODEL), 1.0),
        'state_gla': nrm(4, (DEPTH, DEC_BATCH, GLA_HEADS, GLA_DK, GLA_DV), 0.5),
        'state_gdn': nrm(5, (DEPTH, DEC_BATCH, GDN_HEADS, GDN_DK, GDN_DV), 0.1),
        'state_conv': nrm(6, (DEPTH, DEC_BATCH, CONV_W - 1, GDN_CONV_DIM), 1.0),
        'w_ada': nrm(7, (DEPTH, D_MODEL, 6 * D_MODEL), 0.5 * D_MODEL ** -0.5),
        'b_ada': nrm(8, (DEPTH, 6 * D_MODEL), 0.02),
        'norm_mix': 1.0 + nrm(9, (DEPTH, D_MODEL), 0.02),
        'w_in': nrm(10, (DEPTH, D_MODEL, PROJ_DIM), D_MODEL ** -0.5),
        'w_gla_gate': nrm(11, (DEPTH, GLA_LOWRANK, GLA_HEADS * GLA_DK), GLA_LOWRANK ** -0.5),
        'b_gla_gate': nrm(13, (DEPTH, GLA_HEADS * GLA_DK), 0.02),
        'gla_norm': 1.0 + nrm(14, (DEPTH, GLA_DV), 0.02),
        'w_conv': nrm(15, (DEPTH, CONV_W, GDN_CONV_DIM), 0.5),
        'a_log': jnp.log(jax.random.uniform(ks[16], (DEPTH, GDN_HEADS), minval=1.0, maxval=16.0)),
        'dt_bias': dt + jnp.log(-jnp.expm1(-dt)),
        'gdn_norm': 1.0 + nrm(17, (DEPTH, GDN_DV), 0.02),
        'w_out': nrm(18, (DEPTH, MIX_WIDTH, D_MODEL), MIX_WIDTH ** -0.5),
        'norm_ffn': 1.0 + nrm(19, (DEPTH, D_MODEL), 0.02),
        'w_ff_gate': nrm(20, (N_DENSE, D_MODEL, D_FF_DENSE), D_MODEL ** -0.5),
        'w_ff_up': nrm(21, (N_DENSE, D_MODEL, D_FF_DENSE), D_MODEL ** -0.5),
        'w_ff_down': nrm(22, (N_DENSE, D_FF_DENSE, D_MODEL), D_FF_DENSE ** -0.5),
        'w_router': nrm(23, (N_MOE, D_MODEL, N_EXPERTS), D_MODEL ** -0.5),
        'w_exp_gate': nrm(24, (N_MOE, N_EXPERTS, D_MODEL, D_FF_EXPERT), D_MODEL ** -0.5),
        'w_exp_up': nrm(25, (N_MOE, N_EXPERTS, D_MODEL, D_FF_EXPERT), D_MODEL ** -0.5),
        'w_exp_down': nrm(26, (N_MOE, N_EXPERTS, D_FF_EXPERT, D_MODEL), D_FF_EXPERT ** -0.5),
        'w_ada_final': nrm(27, (D_MODEL, 2 * D_MODEL), 0.5 * D_MODEL ** -0.5),
        'b_ada_final': nrm(28, (2 * D_MODEL,), 0.02),
        'norm_final': 1.0 + nrm(29, (D_MODEL,), 0.02),
    }


def reference(x_prompt, x_sample, c_prompt, c_sample, state_gla, state_gdn, state_conv,
              w_ada, b_ada, norm_mix, w_in, w_gla_gate, b_gla_gate, gla_norm, w_conv, a_log, dt_bias,
              gdn_norm, w_out, norm_ffn, w_ff_gate, w_ff_up, w_ff_down, w_router, w_exp_gate, w_exp_up,
              w_exp_down, w_ada_final, b_ada_final, norm_final):

    def trunk(x, c, s_gla, s_gdn, s_conv):
        new_gla, new_gdn, new_conv = [], [], []
        cs = jax.nn.silu(c)
        for l in range(DEPTH):
            mod = (cs @ w_ada[l] + b_ada[l])[:, None, :]
            sh1, sc1, gt1, sh2, sc2, gt2 = jnp.split(mod, 6, axis=-1)
            h = rmsnorm(x, norm_mix[l]) * (1 + sc1) + sh1
            mix, sa, sb, sc_ = mixer(h, s_gla[l], s_gdn[l], s_conv[l], w_in[l], w_gla_gate[l], b_gla_gate[l],
                                     gla_norm[l], w_conv[l], a_log[l], dt_bias[l], gdn_norm[l], w_out[l])
            x = x + gt1 * mix
            h = rmsnorm(x, norm_ffn[l]) * (1 + sc2) + sh2
            i = l // 2
            if l % 2 == 0:
                f = swiglu(h, w_ff_gate[i], w_ff_up[i], w_ff_down[i])
            else:
                f = moe(h, w_router[i], w_exp_gate[i], w_exp_up[i], w_exp_down[i])
            x = x + gt2 * f
            new_gla.append(sa)
            new_gdn.append(sb)
            new_conv.append(sc_)
        shf, scf = jnp.split((cs @ w_ada_final + b_ada_final)[:, None, :], 2, axis=-1)
        y = rmsnorm(x, norm_final) * (1 + scf) + shf
        return y, jnp.stack(new_gla), jnp.stack(new_gdn), jnp.stack(new_conv)

    bp = x_prompt.shape[0]
    dt_ = x_prompt.dtype
    z_gla = jnp.zeros((DEPTH, bp, GLA_HEADS, GLA_DK, GLA_DV), dt_)
    z_gdn = jnp.zeros((DEPTH, bp, GDN_HEADS, GDN_DK, GDN_DV), dt_)
    z_conv = jnp.zeros((DEPTH, bp, CONV_W - 1, GDN_CONV_DIM), dt_)
    y_prompt, gla_p, gdn_p, conv_p = trunk(x_prompt, c_prompt, z_gla, z_gdn, z_conv)
    y_sample, gla_s, gdn_s, conv_s = trunk(x_sample, c_sample, state_gla, state_gdn, state_conv)
    return (y_prompt, y_sample, gla_p, gdn_p, conv_p, gla_s, gdn_s, conv_s)
```

```python
import functools
import math

import jax
import jax.numpy as jnp
from jax import lax
from jax.experimental import pallas as pl
from jax.experimental.pallas import tpu as pltpu

F32 = jnp.float32
BF16 = jnp.bfloat16

D_MODEL = 1024
GLA_HEADS = 4
GLA_DK = 64
GLA_DV = 128
GLA_LOWRANK = 16
GLA_GATE_NORMALIZER = 16.0
GDN_HEADS = 4
GDN_DK = 128
GDN_DV = 128
CONV_W = 4
GDN_CONV_DIM = GDN_HEADS * (2 * GDN_DK + GDN_DV)
N_EXPERTS = 8
EPS = 1e-6

LANES = 128
SUBLANES = 8
BLOCK = 128
GLA_KPAD = 128
GLA_BAND = 60.0
VMEM_LIMIT = 56 * 1024 * 1024

C_QKV = 0
C_Z = C_QKV + GDN_CONV_DIM
C_Q1 = C_Z + GDN_HEADS * GDN_DV
C_K1 = C_Q1 + GLA_HEADS * GLA_KPAD
C_V1 = C_K1 + GLA_HEADS * GLA_KPAD
C_G1 = C_V1 + GLA_HEADS * GLA_DV
C_MISC = C_G1 + GLA_HEADS * GLA_DV
PROJ_COLS = C_MISC + LANES
MISC_A = GLA_LOWRANK
MISC_B = GLA_LOWRANK + GDN_HEADS


def _dot(a, b, precision=None):
    return jnp.dot(a, b, preferred_element_type=F32, precision=precision)


def _dot_nt(a, b):
    return lax.dot_general(a, b, (((1,), (1,)), ((), ())), preferred_element_type=F32)


def _dot_tn(a, b):
    return lax.dot_general(a, b, (((0,), (0,)), ((), ())), preferred_element_type=F32)


def _mxu_pair(a, b):
    a, b = a.astype(BF16), b.astype(BF16)
    if a.shape[0] % (2 * SUBLANES) != 0 or b.shape[0] % (2 * SUBLANES) != 0:
        return a.astype(F32), b.astype(F32)
    return a, b


def _silu(x):
    return x * (1.0 / (1.0 + jnp.exp(-x)))


def _sigmoid(x):
    return 1.0 / (1.0 + jnp.exp(-x))


def _log_sigmoid(x):
    return jnp.minimum(x, 0.0) - jnp.log1p(jnp.exp(-jnp.abs(x)))


def _softplus(x):
    return jnp.maximum(x, 0.0) + jnp.log1p(jnp.exp(-jnp.abs(x)))


def _rms(x):
    return x * lax.rsqrt(jnp.mean(x * x, axis=-1, keepdims=True) + EPS)


def _seg_cumsum(x, seg_len):
    pos = lax.broadcasted_iota(jnp.int32, x.shape, 0) & (seg_len - 1)
    s = 1
    while s < seg_len:
        x = x + jnp.where(pos >= s, pltpu.roll(x, s, axis=0), 0.0)
        s *= 2
    return x


def _seg_masks(seg_len):
    row = lax.broadcasted_iota(jnp.int32, (BLOCK, BLOCK), 0)
    col = lax.broadcasted_iota(jnp.int32, (BLOCK, BLOCK), 1)
    shift = int(math.log2(seg_len))
    same = (row >> shift) == (col >> shift)
    return same & (row >= col), same & (row > col), row == col


def _ada_body(c_ref, w_ref, b_ref, o_ref):
    cs = _silu(c_ref[...]).astype(BF16)
    o_ref[...] = _dot(cs, w_ref[...].astype(BF16)) + b_ref[...]


def _ada(c_all, w, b):
    rows = c_all.shape[0]
    nout = w.shape[1]
    tn = min(nout, 1024)
    return pl.pallas_call(
        _ada_body,
        grid=(nout // tn,),
        in_specs=[pl.BlockSpec((rows, D_MODEL), lambda j: (0, 0)),
                  pl.BlockSpec((D_MODEL, tn), lambda j: (0, j)),
                  pl.BlockSpec((1, tn), lambda j: (0, j))],
        out_specs=pl.BlockSpec((rows, tn), lambda j: (0, j)),
        out_shape=jax.ShapeDtypeStruct((rows, nout), F32),
        compiler_params=pltpu.CompilerParams(dimension_semantics=("arbitrary",)),
        name="ada_mod",
    )(c_all, w, b.reshape(1, nout))


class _Tokens:
    def __init__(self, n_seq, seq_len, tm):
        self.n_seq, self.seq_len, self.tm = n_seq, seq_len, tm
        self.n = n_seq * seq_len
        self.per_seq = seq_len % tm == 0
        self.tiles = self.n // tm

    def mod_operand(self, mod):
        if self.per_seq:
            return mod.reshape(self.n_seq, 1, mod.shape[1])
        return jnp.repeat(mod, self.seq_len, axis=0)

    def mod_spec(self, col):
        if self.per_seq:
            tps = self.seq_len // self.tm
            return pl.BlockSpec((None, 1, D_MODEL), lambda i: (i // tps, 0, col))
        return pl.BlockSpec((self.tm, D_MODEL), lambda i: (i, col))

    def row_spec(self, width, col=0):
        return pl.BlockSpec((self.tm, width), lambda i: (i, col))


def _const_spec(shape):
    nd = len(shape)
    return pl.BlockSpec(shape, lambda i: (0,) * nd, pipeline_mode=pl.Buffered(1))


def _inproj_body(x_ref, sh_ref, sc_ref, nw_ref, w_ref, o_ref):
    h = _rms(x_ref[...]) * nw_ref[...]
    h = h * (1.0 + sc_ref[...]) + sh_ref[...]
    o_ref[...] = _dot(h.astype(BF16), w_ref[...])


def _inproj(tok, x, mod, nw, w):
    return pl.pallas_call(
        _inproj_body,
        grid=(tok.tiles,),
        in_specs=[tok.row_spec(D_MODEL), tok.mod_spec(0), tok.mod_spec(1),
                  _const_spec((1, D_MODEL)), _const_spec((D_MODEL, PROJ_COLS))],
        out_specs=tok.row_spec(PROJ_COLS),
        out_shape=jax.ShapeDtypeStruct((tok.n, PROJ_COLS), F32),
        compiler_params=pltpu.CompilerParams(dimension_semantics=("arbitrary",),
                                             vmem_limit_bytes=VMEM_LIMIT),
        name="in_proj",
    )(x, mod, mod, nw, w)


class _Mixer:
    def __init__(self, n_seq, seq_len):
        self.n_seq, self.seq_len = n_seq, seq_len
        self.carry = seq_len >= BLOCK
        if self.carry:
            self.nblk = 2
            self.rows = self.nblk * BLOCK
            self.grid = (n_seq, seq_len // self.rows)
            self.seqs = 1
        else:
            self.nblk = 1
            self.rows = BLOCK
            self.seqs = BLOCK // seq_len
            self.grid = (n_seq // self.seqs, 1)
        steps = self.grid[1]
        self.row_map = lambda b, t: b * steps + t

    def rows_spec(self, width, col):
        return pl.BlockSpec((self.rows, width), lambda b, t: (self.row_map(b, t), col))

    def state_spec(self, shape):
        nd = len(shape)
        return pl.BlockSpec((self.seqs,) + shape, lambda b, t: (b,) + (0,) * nd)

    def conv_spec(self):
        return pl.BlockSpec((self.seqs * SUBLANES, GDN_CONV_DIM), lambda b, t: (b, 0))


def _mixer_const(shape):
    nd = len(shape)
    return pl.BlockSpec(shape, lambda b, t: (0,) * nd)


def _gla_body(q_ref, k_ref, v_ref, g_ref, misc_ref, wg_ref, bg_ref, nw_ref, s0_ref,
              o_ref, sout_ref, c_scr, a_scr, *, seg_len, carry, nblk):
    nseg = BLOCK // seg_len
    incl, _, _ = _seg_masks(seg_len)
    if carry:
        @pl.when(pl.program_id(1) == 0)
        def _():
            sout_ref[...] = s0_ref[...]

    for blk in range(nblk):
        r0 = blk * BLOCK
        pre = _dot(misc_ref[r0:r0 + BLOCK, :], wg_ref[...], precision=lax.Precision.HIGHEST) + bg_ref[...]
        la = _log_sigmoid(pre) * (1.0 / GLA_GATE_NORMALIZER)
        c = _seg_cumsum(la, seg_len)
        c_scr[...] = c
        n_bands = (jnp.max(-c) * (1.0 / GLA_BAND)).astype(jnp.int32) + 1

        q_dec, outs = [], []
        for h in range(GLA_HEADS):
            ks = slice(h * GLA_KPAD, (h + 1) * GLA_KPAD)
            ch = c[:, ks]
            qh = q_ref[r0:r0 + BLOCK, ks] * (GLA_DK ** -0.5)
            kh = k_ref[r0:r0 + BLOCK, ks]
            qd = qh * jnp.exp(ch)
            in0 = ch > -GLA_BAND
            k0 = jnp.where(in0, kh * jnp.exp(jnp.where(in0, -ch, 0.0)), 0.0).astype(BF16)
            a_scr[h] = _dot_nt(qd.astype(BF16), k0)
            q_dec.append(qd)

        def band_body(r, carry_):
            lo = r.astype(F32) * GLA_BAND
            cc = c_scr[...]
            for h in range(GLA_HEADS):
                ks = slice(h * GLA_KPAD, (h + 1) * GLA_KPAD)
                ch = cc[:, ks]
                qh = q_ref[r0:r0 + BLOCK, ks] * (GLA_DK ** -0.5)
                kh = k_ref[r0:r0 + BLOCK, ks]
                qb = (qh * jnp.exp(jnp.minimum(ch + lo, 0.0))).astype(BF16)
                inb = (ch <= -lo) & (ch > -(lo + GLA_BAND))
                kb = jnp.where(inb, kh * jnp.exp(jnp.where(inb, -ch - lo, 0.0)), 0.0).astype(BF16)
                a_scr[h] = a_scr[h] + _dot_nt(qb, kb)
            return carry_

        lax.fori_loop(1, n_bands, band_body, 0)

        for h in range(GLA_HEADS):
            ks = slice(h * GLA_KPAD, (h + 1) * GLA_KPAD)
            vs = slice(h * GLA_DV, (h + 1) * GLA_DV)
            ch = c[:, ks]
            kh = k_ref[r0:r0 + BLOCK, ks]
            vh = v_ref[r0:r0 + BLOCK, vs]
            att = jnp.where(incl, a_scr[h], 0.0).astype(BF16)
            o_h = _dot(att, vh.astype(BF16))
            o_inter = []
            for sg in range(nseg):
                rs = slice(sg * seg_len, (sg + 1) * seg_len)
                seq = 0 if carry else blk * nseg + sg
                c_last = ch[(sg + 1) * seg_len - 1:(sg + 1) * seg_len, :]
                st = sout_ref[seq, h] if carry else s0_ref[seq, h]
                o_inter.append(_dot_nt(*_mxu_pair(q_dec[h][rs], st)))
                k_dec = kh[rs] * jnp.exp(c_last - ch[rs])
                sout_ref[seq, h] = st * jnp.exp(c_last) + _dot_tn(*_mxu_pair(vh[rs], k_dec))
            o_h = o_h + (o_inter[0] if nseg == 1 else jnp.concatenate(o_inter, axis=0))
            gate = _silu(g_ref[r0:r0 + BLOCK, vs])
            o_ref[r0:r0 + BLOCK, vs] = _rms(o_h) * nw_ref[...] * gate


def _gla(mx, proj, wg, bg, nw, s0):
    shape = (GLA_HEADS, GLA_DV, GLA_KPAD)
    body = functools.partial(_gla_body, seg_len=min(mx.seq_len, BLOCK), carry=mx.carry, nblk=mx.nblk)
    hk = GLA_HEADS * GLA_KPAD
    hv = GLA_HEADS * GLA_DV
    return pl.pallas_call(
        body,
        grid=mx.grid,
        in_specs=[mx.rows_spec(hk, C_Q1 // hk), mx.rows_spec(hk, C_K1 // hk),
                  mx.rows_spec(hv, C_V1 // hv), mx.rows_spec(hv, C_G1 // hv),
                  mx.rows_spec(LANES, C_MISC // LANES),
                  _mixer_const((LANES, hk)), _mixer_const((1, hk)), _mixer_const((1, GLA_DV)),
                  mx.state_spec(shape)],
        out_specs=[mx.rows_spec(hv, 0), mx.state_spec(shape)],
        out_shape=[jax.ShapeDtypeStruct((mx.n_seq * mx.seq_len, hv), F32),
                   jax.ShapeDtypeStruct((mx.n_seq,) + shape, F32)],
        scratch_shapes=[pltpu.VMEM((BLOCK, hk), F32), pltpu.VMEM((GLA_HEADS, BLOCK, BLOCK), F32)],
        compiler_params=pltpu.CompilerParams(dimension_semantics=("arbitrary", "arbitrary"),
                                             vmem_limit_bytes=VMEM_LIMIT),
        name="gla_mixer",
    )(proj, proj, proj, proj, proj, wg, bg, nw, s0)


def _gdn_body(x_ref, z_ref, misc_ref, wc_ref, alog_ref, dtb_ref, nw_ref, cin_ref, s0_ref,
              o_ref, cout_ref, sout_ref, prev_scr, act_scr, *, seg_len, carry, nblk):
    nseg = BLOCK // seg_len
    rows = nblk * BLOCK
    incl, strict, eye = _seg_masks(seg_len)
    eye_f = eye.astype(F32)
    row = lax.broadcasted_iota(jnp.int32, (BLOCK, BLOCK), 0)
    col = lax.broadcasted_iota(jnp.int32, (BLOCK, BLOCK), 1)
    off_masks = [((row >> (s + 1)) == (col >> (s + 1))) & ((row >> s) != (col >> s)) & (row > col)
                 for s in range(int(math.log2(seg_len)))]
    if carry:
        @pl.when(pl.program_id(1) == 0)
        def _():
            sout_ref[...] = s0_ref[...]
            prev_scr[...] = cin_ref[...]

    x = x_ref[...]
    row = lax.broadcasted_iota(jnp.int32, x.shape, 0)
    if carry:
        before = jnp.concatenate([prev_scr[...], x[SUBLANES:]], axis=0)
        pos = row
    else:
        before = cin_ref[...]
        pos = row & (SUBLANES - 1)
    conv = x * wc_ref[CONV_W - 1:CONV_W, :]
    for d in range(1, CONV_W):
        tap = jnp.where(pos >= d, pltpu.roll(x, d, axis=0), pltpu.roll(before, rows - SUBLANES + d, axis=0))
        conv = conv + tap * wc_ref[CONV_W - 1 - d:CONV_W - d, :]
    act_scr[...] = _silu(conv)
    if carry:
        prev_scr[...] = x[rows - SUBLANES:]
        cout_ref[...] = x[rows - SUBLANES:]
    else:
        cout_ref[...] = x

    for blk in range(nblk):
        r0 = blk * BLOCK
        misc = misc_ref[r0:r0 + BLOCK, :]
        g = -jnp.exp(alog_ref[...]) * _softplus(misc + dtb_ref[...])
        beta_all = _sigmoid(misc)
        gam = _seg_cumsum(g, seg_len)
        gam_t = gam.T
        for h in range(GDN_HEADS):
            hs = slice(h * GDN_DK, (h + 1) * GDN_DK)
            qh = act_scr[r0:r0 + BLOCK, hs]
            kh = act_scr[r0:r0 + BLOCK, GDN_HEADS * GDN_DK + h * GDN_DK:GDN_HEADS * GDN_DK + (h + 1) * GDN_DK]
            vh = act_scr[r0:r0 + BLOCK, 2 * GDN_HEADS * GDN_DK + h * GDN_DV:2 * GDN_HEADS * GDN_DK + (h + 1) * GDN_DV]
            qh = qh * lax.rsqrt(jnp.sum(qh * qh, axis=-1, keepdims=True) + EPS) * (GDN_DK ** -0.5)
            kh = kh * lax.rsqrt(jnp.sum(kh * kh, axis=-1, keepdims=True) + EPS)
            gcol = gam[:, MISC_A + h:MISC_A + h + 1]
            grow = gam_t[MISC_A + h:MISC_A + h + 1, :]
            beta = beta_all[:, MISC_B + h:MISC_B + h + 1]
            egam = jnp.exp(gcol)
            kb = kh.astype(BF16)
            kq = _dot_nt(jnp.concatenate([kb, qh.astype(BF16)], axis=0), kb)
            lmask = jnp.exp(jnp.where(incl, gcol - grow, -1e30))
            m = jnp.where(strict, beta * kq[:BLOCK] * lmask, 0.0)
            a_qk = (kq[BLOCK:] * lmask).astype(BF16)
            inv = eye_f - jnp.where(off_masks[0], m, 0.0)
            for off in off_masks[1:]:
                tb = inv.astype(BF16)
                inv = inv - _dot(_dot(tb, jnp.where(off, m, 0.0).astype(BF16)).astype(BF16), tb)
            rhs = jnp.concatenate([vh * beta, kh * (beta * egam)], axis=1).astype(BF16)
            uw = _dot(inv.astype(BF16), rhs)
            u, w = uw[:, :GDN_DV], uw[:, GDN_DV:]
            q_dec = qh * egam
            v_new, o_inter = [], []
            for sg in range(nseg):
                rs = slice(sg * seg_len, (sg + 1) * seg_len)
                seq = 0 if carry else blk * nseg + sg
                g_last = gcol[(sg + 1) * seg_len - 1:(sg + 1) * seg_len, :]
                st = sout_ref[seq, h] if carry else s0_ref[seq, h]
                wq = _dot(*_mxu_pair(jnp.concatenate([w[rs], q_dec[rs]], axis=0), st))
                vn = u[rs] - wq[:seg_len]
                v_new.append(vn)
                o_inter.append(wq[seg_len:])
                k_dec = kh[rs] * jnp.exp(g_last - gcol[rs])
                sout_ref[seq, h] = st * jnp.exp(g_last) + _dot_tn(*_mxu_pair(k_dec, vn))
            if nseg > 1:
                v_new, o_inter = jnp.concatenate(v_new, axis=0), jnp.concatenate(o_inter, axis=0)
            else:
                v_new, o_inter = v_new[0], o_inter[0]
            o_h = o_inter + _dot(a_qk, v_new.astype(BF16))
            vs = slice(h * GDN_DV, (h + 1) * GDN_DV)
            o_ref[r0:r0 + BLOCK, vs] = _rms(o_h) * nw_ref[...] * _silu(z_ref[r0:r0 + BLOCK, vs])


def _gdn(mx, proj, wc, alog, dtb, nw, cin, s0):
    shape = (GDN_HEADS, GDN_DK, GDN_DV)
    body = functools.partial(_gdn_body, seg_len=min(mx.seq_len, BLOCK // 2), carry=mx.carry, nblk=mx.nblk)
    hv = GDN_HEADS * GDN_DV
    return pl.pallas_call(
        body,
        grid=mx.grid,
        in_specs=[mx.rows_spec(GDN_CONV_DIM, 0), mx.rows_spec(hv, C_Z // hv),
                  mx.rows_spec(LANES, C_MISC // LANES),
                  _mixer_const((CONV_W, GDN_CONV_DIM)), _mixer_const((1, LANES)), _mixer_const((1, LANES)),
                  _mixer_const((1, GDN_DV)), mx.conv_spec(), mx.state_spec(shape)],
        out_specs=[mx.rows_spec(hv, 0), mx.conv_spec(), mx.state_spec(shape)],
        out_shape=[jax.ShapeDtypeStruct((mx.n_seq * mx.seq_len, hv), F32),
                   jax.ShapeDtypeStruct((mx.n_seq * SUBLANES, GDN_CONV_DIM), F32),
                   jax.ShapeDtypeStruct((mx.n_seq,) + shape, F32)],
        scratch_shapes=[pltpu.VMEM((SUBLANES, GDN_CONV_DIM), F32),
                        pltpu.VMEM((mx.rows, GDN_CONV_DIM), F32)],
        compiler_params=pltpu.CompilerParams(dimension_semantics=("arbitrary", "arbitrary"),
                                             vmem_limit_bytes=VMEM_LIMIT),
        name="gdn_mixer",
    )(proj, proj, proj, wc, alog, dtb, nw, cin, s0)


def _split_bf16(x):
    hi = x.astype(BF16)
    return hi, (x - hi.astype(F32)).astype(BF16)


def _post_body(x_ref, o1_ref, o2_ref, wo_ref, gt_ref, sh_ref, sc_ref, nw_ref, *rest, router):
    if router:
        wr_hi_ref, wr_lo_ref, xo_ref, h_ref, comb_ref = rest
    else:
        xo_ref, h_ref = rest
    half = GLA_HEADS * GLA_DV
    mix = _dot(o1_ref[...].astype(BF16), wo_ref[:half, :]) + _dot(o2_ref[...].astype(BF16), wo_ref[half:, :])
    x = x_ref[...] + gt_ref[...] * mix
    xo_ref[...] = x
    h = _rms(x) * nw_ref[...]
    h = h * (1.0 + sc_ref[...]) + sh_ref[...]
    h_ref[...] = h.astype(BF16)
    if router:
        h_hi, h_lo = _split_bf16(h)
        logits = _dot(h_hi, wr_hi_ref[...]) + (_dot(h_hi, wr_lo_ref[...]) + _dot(h_lo, wr_hi_ref[...]))
        lane = lax.broadcasted_iota(jnp.int32, logits.shape, 1)
        lg = jnp.where(lane < N_EXPERTS, logits, -jnp.inf)
        m1 = jnp.max(lg, axis=-1, keepdims=True)
        i1 = jnp.min(jnp.where(lg == m1, lane, LANES), axis=-1, keepdims=True)
        lg2 = jnp.where(lane == i1, -jnp.inf, lg)
        m2 = jnp.max(lg2, axis=-1, keepdims=True)
        i2 = jnp.min(jnp.where(lg2 == m2, lane, LANES), axis=-1, keepdims=True)
        e2 = jnp.exp(m2 - m1)
        w1 = 1.0 / (1.0 + e2)
        comb_ref[...] = jnp.where(lane == i1, w1, 0.0) + jnp.where(lane == i2, e2 * w1, 0.0)


def _post(tok, x, o1, o2, wo, mod, nw, wr=None):
    router = wr is not None
    half = GLA_HEADS * GLA_DV
    in_specs = [tok.row_spec(D_MODEL), tok.row_spec(half), tok.row_spec(half),
                _const_spec((D_MODEL, D_MODEL)), tok.mod_spec(2), tok.mod_spec(3), tok.mod_spec(4),
                _const_spec((1, D_MODEL))]
    args = [x, o1, o2, wo, mod, mod, mod, nw]
    out_specs = [tok.row_spec(D_MODEL), tok.row_spec(D_MODEL)]
    out_shape = [jax.ShapeDtypeStruct((tok.n, D_MODEL), F32), jax.ShapeDtypeStruct((tok.n, D_MODEL), BF16)]
    if router:
        in_specs += [_const_spec((D_MODEL, LANES)), _const_spec((D_MODEL, LANES))]
        args += list(wr)
        out_specs.append(tok.row_spec(LANES))
        out_shape.append(jax.ShapeDtypeStruct((tok.n, LANES), F32))
    return pl.pallas_call(
        functools.partial(_post_body, router=router),
        grid=(tok.tiles,),
        in_specs=in_specs, out_specs=out_specs, out_shape=out_shape,
        compiler_params=pltpu.CompilerParams(dimension_semantics=("arbitrary",),
                                             vmem_limit_bytes=VMEM_LIMIT),
        name="post_mixer",
    )(*args)


def _ffn_body(h_ref, x_ref, gt_ref, wg_ref, wu_ref, wd_ref, o_ref, *, chunk):
    h = h_ref[...]
    d_ff = wg_ref.shape[1]
    acc = None
    for lo in range(0, d_ff, chunk):
        g = _dot(h, wg_ref[:, lo:lo + chunk])
        u = _dot(h, wu_ref[:, lo:lo + chunk])
        part = _dot((_silu(g) * u).astype(BF16), wd_ref[lo:lo + chunk, :])
        acc = part if acc is None else acc + part
    o_ref[...] = x_ref[...] + gt_ref[...] * acc


def _ffn(tok, h, x, mod, wg, wu, wd):
    d_ff = wg.shape[1]
    return pl.pallas_call(
        functools.partial(_ffn_body, chunk=d_ff // 2),
        grid=(tok.tiles,),
        in_specs=[tok.row_spec(D_MODEL), tok.row_spec(D_MODEL), tok.mod_spec(5),
                  _const_spec((D_MODEL, d_ff)), _const_spec((D_MODEL, d_ff)), _const_spec((d_ff, D_MODEL))],
        out_specs=tok.row_spec(D_MODEL),
        out_shape=jax.ShapeDtypeStruct((tok.n, D_MODEL), F32),
        compiler_params=pltpu.CompilerParams(dimension_semantics=("arbitrary",),
                                             vmem_limit_bytes=VMEM_LIMIT),
        name="ffn_dense",
    )(h, x, mod, wg, wu, wd)


def _moe_body(h_ref, comb_ref, x_ref, gt_ref, wg_ref, wu_ref, wd_ref, o_ref, acc_ref):
    e, j = pl.program_id(1), pl.program_id(2)

    @pl.when((e == 0) & (j == 0))
    def _():
        acc_ref[...] = jnp.zeros_like(acc_ref)

    h = h_ref[...]
    a = (_silu(_dot(h, wg_ref[...])) * _dot(h, wu_ref[...])).astype(BF16)
    y = _dot(a, wd_ref[...])
    sel = (lax.broadcasted_iota(jnp.int32, (LANES, LANES), 0) == e).astype(F32)
    cw = _dot(comb_ref[...], sel, precision=lax.Precision.HIGHEST)
    acc_ref[...] += jnp.concatenate([cw] * (D_MODEL // LANES), axis=1) * y

    @pl.when((e == pl.num_programs(1) - 1) & (j == pl.num_programs(2) - 1))
    def _():
        o_ref[...] = x_ref[...] + gt_ref[...] * acc_ref[...]


def _moe(tok, h, comb, x, mod, wg, wu, wd, tf):
    d_ff = wg.shape[2]
    if tok.per_seq:
        tps = tok.seq_len // tok.tm
        gt_spec = pl.BlockSpec((None, 1, D_MODEL), lambda i, e, j: (i // tps, 0, 5))
    else:
        gt_spec = pl.BlockSpec((tok.tm, D_MODEL), lambda i, e, j: (i, 5))
    row = lambda w: pl.BlockSpec((tok.tm, w), lambda i, e, j: (i, 0))
    return pl.pallas_call(
        _moe_body,
        grid=(tok.tiles, N_EXPERTS, d_ff // tf),
        in_specs=[row(D_MODEL), row(LANES), row(D_MODEL), gt_spec,
                  pl.BlockSpec((None, D_MODEL, tf), lambda i, e, j: (e, 0, j)),
                  pl.BlockSpec((None, D_MODEL, tf), lambda i, e, j: (e, 0, j)),
                  pl.BlockSpec((None, tf, D_MODEL), lambda i, e, j: (e, j, 0))],
        out_specs=row(D_MODEL),
        out_shape=jax.ShapeDtypeStruct((tok.n, D_MODEL), F32),
        scratch_shapes=[pltpu.VMEM((tok.tm, D_MODEL), F32)],
        compiler_params=pltpu.CompilerParams(dimension_semantics=("arbitrary", "arbitrary", "arbitrary"),
                                             vmem_limit_bytes=VMEM_LIMIT),
        name="moe_dense",
    )(h, comb, x, mod, wg, wu, wd)


def _final_body(x_ref, sh_ref, sc_ref, nw_ref, o_ref):
    h = _rms(x_ref[...]) * nw_ref[...]
    o_ref[...] = h * (1.0 + sc_ref[...]) + sh_ref[...]


def _final(tok, x, mod, nw):
    return pl.pallas_call(
        _final_body,
        grid=(tok.tiles,),
        in_specs=[tok.row_spec(D_MODEL), tok.mod_spec(0), tok.mod_spec(1), _const_spec((1, D_MODEL))],
        out_specs=tok.row_spec(D_MODEL),
        out_shape=jax.ShapeDtypeStruct((tok.n, D_MODEL), F32),
        compiler_params=pltpu.CompilerParams(dimension_semantics=("arbitrary",)),
        name="final_norm",
    )(x, mod, mod, nw)


def _pad_heads(w, heads, dk, kpad):
    lead = w.shape[:-1]
    w = w.reshape(lead + (heads, dk))
    w = jnp.pad(w, [(0, 0)] * len(lead) + [(0, 0), (0, kpad - dk)])
    return w.reshape(lead + (heads * kpad,))


def _layout_w_in(w):
    hk, hv = GLA_HEADS * GLA_DK, GLA_HEADS * GLA_DV
    o = 0
    q1 = w[:, o:o + hk]; o += hk
    k1 = w[:, o:o + hk]; o += hk
    v1 = w[:, o:o + hv]; o += hv
    lr = w[:, o:o + GLA_LOWRANK]; o += GLA_LOWRANK
    g1 = w[:, o:o + hv]; o += hv
    qkv = w[:, o:o + GDN_CONV_DIM]; o += GDN_CONV_DIM
    a = w[:, o:o + GDN_HEADS]; o += GDN_HEADS
    b = w[:, o:o + GDN_HEADS]; o += GDN_HEADS
    z = w[:, o:o + GDN_HEADS * GDN_DV]
    misc = jnp.concatenate([lr, a, b], axis=1)
    misc = jnp.pad(misc, ((0, 0), (0, LANES - misc.shape[1])))
    out = jnp.concatenate([qkv, z, _pad_heads(q1, GLA_HEADS, GLA_DK, GLA_KPAD),
                           _pad_heads(k1, GLA_HEADS, GLA_DK, GLA_KPAD), v1, g1, misc], axis=1)
    return out.astype(BF16)


def _misc_row(v, lane0):
    return jnp.pad(v.astype(F32), (lane0, LANES - lane0 - v.shape[0])).reshape(1, LANES)


def _trunk(x, mods, mod_f, s_gla, s_gdn, s_conv, p):
    n_seq, seq_len, _ = x.shape
    depth = len(mods)
    tok = _Tokens(n_seq, seq_len, 512)
    mx = _Mixer(n_seq, seq_len)
    x = x.reshape(tok.n, D_MODEL)
    new_gla, new_gdn, new_conv = [], [], []
    for l in range(depth):
        mod = tok.mod_operand(mods[l])
        proj = _inproj(tok, x, mod, p["norm_mix"][l], p["w_in"][l])
        s0 = jnp.pad(jnp.swapaxes(s_gla[l], -1, -2), ((0, 0), (0, 0), (0, 0), (0, GLA_KPAD - GLA_DK)))
        o1, sa = _gla(mx, proj, p["w_gate"][l], p["b_gate"][l], p["gla_norm"][l], s0)
        cin = jnp.pad(s_conv[l], ((0, 0), (SUBLANES - (CONV_W - 1), 0), (0, 0))).reshape(n_seq * SUBLANES, GDN_CONV_DIM)
        o2, cout, sb = _gdn(mx, proj, p["w_conv"][l], p["a_log"][l], p["dt_bias"][l], p["gdn_norm"][l], cin, s_gdn[l])
        new_gla.append(jnp.swapaxes(sa[..., :GLA_DK], -1, -2))
        new_gdn.append(sb)
        new_conv.append(cout.reshape(n_seq, SUBLANES, GDN_CONV_DIM)[:, SUBLANES - (CONV_W - 1):])
        i = l // 2
        if l % 2 == 0:
            x, h = _post(tok, x, o1, o2, p["w_out"][l], mod, p["norm_ffn"][l])
            x = _ffn(tok, h, x, mod, p["w_ff_gate"][i], p["w_ff_up"][i], p["w_ff_down"][i])
        else:
            x, h, comb = _post(tok, x, o1, o2, p["w_out"][l], mod, p["norm_ffn"][l], p["w_router"][i])
            x = _moe(tok, h, comb, x, mod, p["w_exp_gate"][i], p["w_exp_up"][i], p["w_exp_down"][i], 896)
    y = _final(tok, x, tok.mod_operand(mod_f), p["norm_final"])
    return (y.reshape(n_seq, seq_len, D_MODEL), jnp.stack(new_gla), jnp.stack(new_gdn), jnp.stack(new_conv))


def kernel(x_prompt, x_sample, c_prompt, c_sample, state_gla, state_gdn, state_conv, w_ada, b_ada, norm_mix, w_in, w_gla_gate, b_gla_gate, gla_norm, w_conv, a_log, dt_bias, gdn_norm, w_out, norm_ffn, w_ff_gate, w_ff_up, w_ff_down, w_router, w_exp_gate, w_exp_up, w_exp_down, w_ada_final, b_ada_final, norm_final):
    depth = w_in.shape[0]
    bp, bs = x_prompt.shape[0], x_sample.shape[0]
    dt = x_prompt.dtype

    wg_pad = jnp.pad(_pad_heads(w_gla_gate, GLA_HEADS, GLA_DK, GLA_KPAD), ((0, 0), (0, LANES - GLA_LOWRANK), (0, 0)))
    pad_lane = 1.0 - _pad_heads(jnp.ones_like(b_gla_gate), GLA_HEADS, GLA_DK, GLA_KPAD)
    bg_pad = _pad_heads(b_gla_gate, GLA_HEADS, GLA_DK, GLA_KPAD) + 30.0 * pad_lane
    wr = jnp.pad(w_router, ((0, 0), (0, 0), (0, LANES - N_EXPERTS)))
    wr_hi = wr.astype(BF16)
    wr_lo = (wr - wr_hi.astype(F32)).astype(BF16)
    p = {
        "norm_mix": norm_mix.reshape(depth, 1, D_MODEL),
        "w_in": jnp.stack([_layout_w_in(w_in[l]) for l in range(depth)]),
        "w_gate": wg_pad, "b_gate": bg_pad.reshape(depth, 1, -1),
        "gla_norm": gla_norm.reshape(depth, 1, GLA_DV),
        "w_conv": w_conv,
        "a_log": jnp.stack([_misc_row(a_log[l], MISC_A) for l in range(depth)]),
        "dt_bias": jnp.stack([_misc_row(dt_bias[l], MISC_A) for l in range(depth)]),
        "gdn_norm": gdn_norm.reshape(depth, 1, GDN_DV),
        "w_out": w_out.astype(BF16),
        "norm_ffn": norm_ffn.reshape(depth, 1, D_MODEL),
        "w_ff_gate": w_ff_gate.astype(BF16), "w_ff_up": w_ff_up.astype(BF16), "w_ff_down": w_ff_down.astype(BF16),
        "w_router": [(wr_hi[i], wr_lo[i]) for i in range(wr.shape[0])],
        "w_exp_gate": w_exp_gate.astype(BF16), "w_exp_up": w_exp_up.astype(BF16), "w_exp_down": w_exp_down.astype(BF16),
        "norm_final": norm_final.reshape(1, D_MODEL),
    }

    c_all = jnp.concatenate([c_prompt, c_sample], axis=0)
    mods = [_ada(c_all, w_ada[l], b_ada[l]) for l in range(depth)]
    mod_f = _ada(c_all, w_ada_final, b_ada_final)
    z_gla = jnp.zeros((depth, bp, GLA_HEADS, GLA_DK, GLA_DV), dt)
    z_gdn = jnp.zeros((depth, bp, GDN_HEADS, GDN_DK, GDN_DV), dt)
    z_conv = jnp.zeros((depth, bp, CONV_W - 1, GDN_CONV_DIM), dt)
    y_p, gla_p, gdn_p, conv_p = _trunk(x_prompt, [m[:bp] for m in mods], mod_f[:bp], z_gla, z_gdn, z_conv, p)
    y_s, gla_s, gdn_s, conv_s = _trunk(x_sample, [m[bp:] for m in mods], mod_f[bp:], state_gla, state_gdn, state_conv, p)
    return (y_p, y_s, gla_p, gdn_p, conv_p, gla_s, gdn_s, conv_s)
```

```python
import functools
import math

import jax
import jax.numpy as jnp
from jax import lax
from jax.experimental import pallas as pl
from jax.experimental.pallas import tpu as pltpu

F32 = jnp.float32
BF16 = jnp.bfloat16

D_MODEL = 1024
GLA_HEADS = 4
GLA_DK = 64
GLA_DV = 128
GLA_LOWRANK = 16
GLA_GATE_NORMALIZER = 16.0
GDN_HEADS = 4
GDN_DK = 128
GDN_DV = 128
CONV_W = 4
GDN_CONV_DIM = GDN_HEADS * (2 * GDN_DK + GDN_DV)
N_EXPERTS = 8
EPS = 1e-6

LANES = 128
SUBLANES = 8
BLOCK = 128
GLA_KPAD = 128
GLA_BAND = 60.0
VMEM_LIMIT = 56 * 1024 * 1024

C_QKV = 0
C_Z = C_QKV + GDN_CONV_DIM
C_Q1 = C_Z + GDN_HEADS * GDN_DV
C_K1 = C_Q1 + GLA_HEADS * GLA_KPAD
C_V1 = C_K1 + GLA_HEADS * GLA_KPAD
C_G1 = C_V1 + GLA_HEADS * GLA_DV
C_MISC = C_G1 + GLA_HEADS * GLA_DV
PROJ_COLS = C_MISC + LANES
MISC_A = GLA_LOWRANK
MISC_B = GLA_LOWRANK + GDN_HEADS
TOP_K = 2
ROUTE_E = 0
ROUTE_W = TOP_K


def _dot(a, b, precision=None):
    return jnp.dot(a, b, preferred_element_type=F32, precision=precision)


def _dot_nt(a, b):
    return lax.dot_general(a, b, (((1,), (1,)), ((), ())), preferred_element_type=F32)


def _dot_tn(a, b):
    return lax.dot_general(a, b, (((0,), (0,)), ((), ())), preferred_element_type=F32)


def _mxu_pair(a, b):
    a, b = a.astype(BF16), b.astype(BF16)
    if a.shape[0] % (2 * SUBLANES) != 0 or b.shape[0] % (2 * SUBLANES) != 0:
        return a.astype(F32), b.astype(F32)
    return a, b


def _silu(x):
    return x * (1.0 / (1.0 + jnp.exp(-x)))


def _sigmoid(x):
    return 1.0 / (1.0 + jnp.exp(-x))


def _log_sigmoid(x):
    return jnp.minimum(x, 0.0) - jnp.log1p(jnp.exp(-jnp.abs(x)))


def _softplus(x):
    return jnp.maximum(x, 0.0) + jnp.log1p(jnp.exp(-jnp.abs(x)))


def _rms(x):
    return x * lax.rsqrt(jnp.mean(x * x, axis=-1, keepdims=True) + EPS)


def _per_token(mod_ref, tm):
    m = mod_ref[...]
    n_seq = m.shape[0]
    if n_seq == 1:
        return m
    shift = int(math.log2(tm // n_seq))
    assert n_seq << shift == tm
    row = lax.broadcasted_iota(jnp.int32, (tm, n_seq), 0)
    col = lax.broadcasted_iota(jnp.int32, (tm, n_seq), 1)
    return _dot(((row >> shift) == col).astype(F32), m, precision=lax.Precision.HIGHEST)


def _seg_cumsum(x, seg_len):
    pos = lax.broadcasted_iota(jnp.int32, x.shape, 0) & (seg_len - 1)
    s = 1
    while s < seg_len:
        x = x + jnp.where(pos >= s, pltpu.roll(x, s, axis=0), 0.0)
        s *= 2
    return x


def _seg_masks(seg_len):
    row = lax.broadcasted_iota(jnp.int32, (BLOCK, BLOCK), 0)
    col = lax.broadcasted_iota(jnp.int32, (BLOCK, BLOCK), 1)
    shift = int(math.log2(seg_len))
    same = (row >> shift) == (col >> shift)
    return same & (row >= col), same & (row > col), row == col


def _ada_body(c_ref, w_ref, b_ref, o_ref):
    cs = _silu(c_ref[...]).astype(BF16)
    o_ref[...] = _dot(cs, w_ref[...].astype(BF16)) + b_ref[...]


def _ada(c_all, w, b):
    rows = c_all.shape[0]
    nout = w.shape[1]
    tn = min(nout, 1024)
    return pl.pallas_call(
        _ada_body,
        grid=(nout // tn,),
        in_specs=[pl.BlockSpec((rows, D_MODEL), lambda j: (0, 0)),
                  pl.BlockSpec((D_MODEL, tn), lambda j: (0, j)),
                  pl.BlockSpec((1, tn), lambda j: (0, j))],
        out_specs=pl.BlockSpec((rows, tn), lambda j: (0, j)),
        out_shape=jax.ShapeDtypeStruct((rows, nout), F32),
        compiler_params=pltpu.CompilerParams(dimension_semantics=("arbitrary",)),
        name="ada_mod",
    )(c_all, w, b.reshape(1, nout))


class _Tokens:
    def __init__(self, n_seq, seq_len, tm):
        self.n_seq, self.seq_len, self.tm = n_seq, seq_len, tm
        self.n = n_seq * seq_len
        self.per_seq = seq_len % tm == 0
        self.tiles = self.n // tm

    def mod_operand(self, mod):
        if self.per_seq:
            return mod.reshape(self.n_seq, 1, mod.shape[1])
        return mod

    def mod_spec(self, col):
        if self.per_seq:
            tps = self.seq_len // self.tm
            return pl.BlockSpec((None, 1, D_MODEL), lambda i: (i // tps, 0, col))
        return pl.BlockSpec((self.tm // self.seq_len, D_MODEL), lambda i: (i, col))

    def row_spec(self, width, col=0):
        return pl.BlockSpec((self.tm, width), lambda i: (i, col))


def _const_spec(shape):
    nd = len(shape)
    return pl.BlockSpec(shape, lambda i: (0,) * nd, pipeline_mode=pl.Buffered(1))


def _inproj_body(x_ref, sh_ref, sc_ref, nw_ref, w_ref, o_ref):
    tm = x_ref.shape[0]
    h = _rms(x_ref[...]) * nw_ref[...]
    h = h * (1.0 + _per_token(sc_ref, tm)) + _per_token(sh_ref, tm)
    o_ref[...] = _dot(h.astype(BF16), w_ref[...])


def _inproj(tok, x, mod, nw, w):
    return pl.pallas_call(
        _inproj_body,
        grid=(tok.tiles,),
        in_specs=[tok.row_spec(D_MODEL), tok.mod_spec(0), tok.mod_spec(1),
                  _const_spec((1, D_MODEL)), _const_spec((D_MODEL, PROJ_COLS))],
        out_specs=tok.row_spec(PROJ_COLS),
        out_shape=jax.ShapeDtypeStruct((tok.n, PROJ_COLS), F32),
        compiler_params=pltpu.CompilerParams(dimension_semantics=("arbitrary",),
                                             vmem_limit_bytes=VMEM_LIMIT),
        name="in_proj",
    )(x, mod, mod, nw, w)


class _Mixer:
    def __init__(self, n_seq, seq_len):
        self.n_seq, self.seq_len = n_seq, seq_len
        self.carry = seq_len >= BLOCK
        if self.carry:
            self.nblk = 4
            self.rows = self.nblk * BLOCK
            self.grid = (n_seq, seq_len // self.rows)
            self.seqs = 1
        else:
            self.nblk = 1
            self.rows = BLOCK
            self.seqs = BLOCK // seq_len
            self.grid = (n_seq // self.seqs, 1)
        steps = self.grid[1]
        self.row_map = lambda b, t: b * steps + t

    def rows_spec(self, width, col):
        return pl.BlockSpec((self.rows, width), lambda b, t: (self.row_map(b, t), col))

    def state_spec(self, shape, layer=None):
        nd = len(shape)
        if layer is None:
            return pl.BlockSpec((self.seqs,) + shape, lambda b, t: (b,) + (0,) * nd)
        return pl.BlockSpec((None, self.seqs) + shape, lambda b, t: (layer, b) + (0,) * nd)

    def conv_spec(self, layer=None):
        if layer is None:
            return pl.BlockSpec((self.seqs * SUBLANES, GDN_CONV_DIM), lambda b, t: (b, 0))
        return pl.BlockSpec((None, self.seqs * SUBLANES, GDN_CONV_DIM), lambda b, t: (layer, b, 0))


def _mixer_const(shape):
    nd = len(shape)
    return pl.BlockSpec(shape, lambda b, t: (0,) * nd)


def _gla_body(q_ref, k_ref, v_ref, g_ref, misc_ref, wg_ref, bg_ref, nw_ref, s0_ref,
              o_ref, sout_ref, c_scr, a_scr, *, seg_len, carry, nblk):
    nseg = BLOCK // seg_len
    assert nseg == 1 or nseg % SUBLANES == 0
    incl, _, _ = _seg_masks(seg_len)
    heads = range(GLA_HEADS)
    if carry:
        @pl.when(pl.program_id(1) == 0)
        def _():
            sout_ref[...] = s0_ref[...]

    for blk in range(nblk):
        r0 = blk * BLOCK
        pre = _dot(misc_ref[r0:r0 + BLOCK, :], wg_ref[...], precision=lax.Precision.HIGHEST) + bg_ref[...]
        la = _log_sigmoid(pre) * (1.0 / GLA_GATE_NORMALIZER)
        c = _seg_cumsum(la, seg_len)
        n_bands = (jnp.max(-c) * (1.0 / GLA_BAND)).astype(jnp.int32) + 1

        q_dec, k_in, v_in, c_in = [], [], [], []
        for h in heads:
            ks = slice(h * GLA_KPAD, (h + 1) * GLA_KPAD)
            ch = c[:, ks]
            c_scr[h] = ch
            qh = q_ref[r0:r0 + BLOCK, ks] * (GLA_DK ** -0.5)
            kh = k_ref[r0:r0 + BLOCK, ks]
            qd = qh * jnp.exp(ch)
            in0 = ch > -GLA_BAND
            k0 = jnp.where(in0, kh * jnp.exp(jnp.where(in0, -ch, 0.0)), 0.0).astype(BF16)
            a_scr[h] = _dot_nt(qd.astype(BF16), k0)
            q_dec.append(qd)
            k_in.append(kh)
            c_in.append(ch)
            v_in.append(v_ref[r0:r0 + BLOCK, h * GLA_DV:(h + 1) * GLA_DV])

        def band_body(r, carry_):
            lo = r.astype(F32) * GLA_BAND
            for h in heads:
                ks = slice(h * GLA_KPAD, (h + 1) * GLA_KPAD)
                ch = c_scr[h]
                qh = q_ref[r0:r0 + BLOCK, ks] * (GLA_DK ** -0.5)
                kh = k_ref[r0:r0 + BLOCK, ks]
                qb = (qh * jnp.exp(jnp.minimum(ch + lo, 0.0))).astype(BF16)
                inb = (ch <= -lo) & (ch > -(lo + GLA_BAND))
                kb = jnp.where(inb, kh * jnp.exp(jnp.where(inb, -ch - lo, 0.0)), 0.0).astype(BF16)
                a_scr[h] = a_scr[h] + _dot_nt(qb, kb)
            return carry_

        lax.fori_loop(1, n_bands, band_body, 0)

        decay_cols = []
        for h in heads:
            c_last = c_scr[h, pl.ds(seg_len - 1, nseg, stride=seg_len), :]
            if nseg == 1:
                c_last = jnp.broadcast_to(c_last, (SUBLANES, GLA_KPAD))
            c_last = jnp.concatenate([c_last, jnp.zeros((BLOCK - c_last.shape[0], GLA_KPAD), F32)], axis=0)
            decay_cols.append(jnp.exp(c_last.T[:GLA_DK, :]))
        o_intra = [_dot(jnp.where(incl, a_scr[h], 0.0).astype(BF16), v_in[h].astype(BF16)) for h in heads]
        o_inter = [[] for _ in heads]
        for sg in range(nseg):
            rs = slice(sg * seg_len, (sg + 1) * seg_len)
            seq = 0 if carry else blk * nseg + sg
            for h in heads:
                c_last = c_in[h][(sg + 1) * seg_len - 1:(sg + 1) * seg_len, :]
                st = sout_ref[seq, h] if carry else s0_ref[seq, h]
                o_inter[h].append(_dot(*_mxu_pair(q_dec[h][rs][:, :GLA_DK], st)))
                k_dec = (k_in[h][rs] * jnp.exp(c_last - c_in[h][rs]))[:, :GLA_DK]
                sout_ref[seq, h] = st * decay_cols[h][:, sg:sg + 1] + _dot_tn(*_mxu_pair(k_dec, v_in[h][rs]))
        for h in heads:
            vs = slice(h * GLA_DV, (h + 1) * GLA_DV)
            o_h = o_intra[h] + (o_inter[h][0] if nseg == 1 else jnp.concatenate(o_inter[h], axis=0))
            o_ref[r0:r0 + BLOCK, vs] = _rms(o_h) * nw_ref[...] * _silu(g_ref[r0:r0 + BLOCK, vs])


def _gla(mx, proj, wg, bg, nw, s0, layer):
    shape = (GLA_HEADS, GLA_DK, GLA_DV)
    body = functools.partial(_gla_body, seg_len=min(mx.seq_len, BLOCK), carry=mx.carry, nblk=mx.nblk)
    hk = GLA_HEADS * GLA_KPAD
    hv = GLA_HEADS * GLA_DV
    return pl.pallas_call(
        body,
        grid=mx.grid,
        in_specs=[mx.rows_spec(hk, C_Q1 // hk), mx.rows_spec(hk, C_K1 // hk),
                  mx.rows_spec(hv, C_V1 // hv), mx.rows_spec(hv, C_G1 // hv),
                  mx.rows_spec(LANES, C_MISC // LANES),
                  _mixer_const((LANES, hk)), _mixer_const((1, hk)), _mixer_const((1, GLA_DV)),
                  mx.state_spec(shape, layer)],
        out_specs=[mx.rows_spec(hv, 0), mx.state_spec(shape)],
        out_shape=[jax.ShapeDtypeStruct((mx.n_seq * mx.seq_len, hv), F32),
                   jax.ShapeDtypeStruct((mx.n_seq,) + shape, F32)],
        scratch_shapes=[pltpu.VMEM((GLA_HEADS, BLOCK, GLA_KPAD), F32), pltpu.VMEM((GLA_HEADS, BLOCK, BLOCK), F32)],
        compiler_params=pltpu.CompilerParams(dimension_semantics=("arbitrary", "arbitrary"),
                                             vmem_limit_bytes=VMEM_LIMIT),
        name="gla_mixer",
    )(proj, proj, proj, proj, proj, wg, bg, nw, s0)


def _gdn_body(x_ref, z_ref, misc_ref, wc_ref, alog_ref, dtb_ref, nw_ref, cin_ref, s0_ref,
              o_ref, cout_ref, sout_ref, prev_scr, act_scr, *, seg_len, carry, nblk):
    nseg = BLOCK // seg_len
    rows = nblk * BLOCK
    incl, strict, eye = _seg_masks(seg_len)
    eye_f = eye.astype(F32)
    row = lax.broadcasted_iota(jnp.int32, (BLOCK, BLOCK), 0)
    col = lax.broadcasted_iota(jnp.int32, (BLOCK, BLOCK), 1)
    off_masks = [((row >> (s + 1)) == (col >> (s + 1))) & ((row >> s) != (col >> s)) & (row > col)
                 for s in range(int(math.log2(seg_len)))]
    if carry:
        @pl.when(pl.program_id(1) == 0)
        def _():
            sout_ref[...] = s0_ref[...]
            prev_scr[...] = cin_ref[...]

    x = x_ref[...]
    conv = x * wc_ref[CONV_W - 1:CONV_W, :]
    for d in range(1, CONV_W):
        tap = pltpu.roll(x, d, axis=0)
        if carry:
            pos = lax.broadcasted_iota(jnp.int32, (SUBLANES, GDN_CONV_DIM), 0)
            head = jnp.where(pos >= d, tap[:SUBLANES], pltpu.roll(prev_scr[...], d, axis=0))
            tap = jnp.concatenate([head, tap[SUBLANES:]], axis=0)
        else:
            pos = lax.broadcasted_iota(jnp.int32, x.shape, 0) & (SUBLANES - 1)
            tap = jnp.where(pos >= d, tap, pltpu.roll(cin_ref[...], rows - SUBLANES + d, axis=0))
        conv = conv + tap * wc_ref[CONV_W - 1 - d:CONV_W - d, :]
    act_scr[...] = _silu(conv)
    if carry:
        prev_scr[...] = x[rows - SUBLANES:]
        cout_ref[...] = x[rows - SUBLANES:]
    else:
        cout_ref[...] = x

    chains = [(blk, h) for blk in range(nblk) for h in range(GDN_HEADS)]
    qs, ks, vs_, gcols, betas, egams, ms, aqks = {}, {}, {}, {}, {}, {}, {}, {}
    for blk in range(nblk):
        r0 = blk * BLOCK
        misc = misc_ref[r0:r0 + BLOCK, :]
        g = -jnp.exp(alog_ref[...]) * _softplus(misc + dtb_ref[...])
        beta_all = _sigmoid(misc)
        gam = _seg_cumsum(g, seg_len)
        gam_t = gam.T
        for h in range(GDN_HEADS):
            c = (blk, h)
            qh = act_scr[r0:r0 + BLOCK, h * GDN_DK:(h + 1) * GDN_DK]
            kh = act_scr[r0:r0 + BLOCK, (GDN_HEADS + h) * GDN_DK:(GDN_HEADS + h + 1) * GDN_DK]
            vs_[c] = act_scr[r0:r0 + BLOCK, 2 * GDN_HEADS * GDN_DK + h * GDN_DV:2 * GDN_HEADS * GDN_DK + (h + 1) * GDN_DV]
            qs[c] = qh * lax.rsqrt(jnp.sum(qh * qh, axis=-1, keepdims=True) + EPS) * (GDN_DK ** -0.5)
            ks[c] = kh * lax.rsqrt(jnp.sum(kh * kh, axis=-1, keepdims=True) + EPS)
            gcols[c] = gam[:, MISC_A + h:MISC_A + h + 1]
            grow = gam_t[MISC_A + h:MISC_A + h + 1, :]
            betas[c] = beta_all[:, MISC_B + h:MISC_B + h + 1]
            egams[c] = jnp.exp(gcols[c])
            kb = ks[c].astype(BF16)
            kq = _dot_nt(jnp.concatenate([kb, qs[c].astype(BF16)], axis=0), kb)
            lmask = jnp.exp(jnp.where(incl, gcols[c] - grow, -1e30))
            ms[c] = jnp.where(strict, betas[c] * kq[:BLOCK] * lmask, 0.0)
            aqks[c] = (kq[BLOCK:] * lmask).astype(BF16)

    invs = {c: eye_f - jnp.where(off_masks[0], ms[c], 0.0) for c in chains}
    for off in off_masks[1:]:
        tbs = {c: invs[c].astype(BF16) for c in chains}
        tms = {c: _dot(tbs[c], jnp.where(off, ms[c], 0.0).astype(BF16)).astype(BF16) for c in chains}
        invs = {c: invs[c] - _dot(tms[c], tbs[c]) for c in chains}
    us, ws, qdecs = {}, {}, {}
    for c in chains:
        rhs = jnp.concatenate([vs_[c] * betas[c], ks[c] * (betas[c] * egams[c])], axis=1).astype(BF16)
        uw = _dot(invs[c].astype(BF16), rhs)
        us[c], ws[c] = uw[:, :GDN_DV], uw[:, GDN_DV:]
        qdecs[c] = qs[c] * egams[c]

    v_new = {c: [] for c in chains}
    o_inter = {c: [] for c in chains}
    for blk in range(nblk):
        for sg in range(nseg):
            rs = slice(sg * seg_len, (sg + 1) * seg_len)
            seq = 0 if carry else blk * nseg + sg
            for h in range(GDN_HEADS):
                c = (blk, h)
                g_last = gcols[c][(sg + 1) * seg_len - 1:(sg + 1) * seg_len, :]
                st = sout_ref[seq, h] if carry else s0_ref[seq, h]
                k_dec = ks[c][rs] * jnp.exp(g_last - gcols[c][rs])
                if carry:
                    kwu = _dot_tn(*_mxu_pair(k_dec, jnp.concatenate([ws[c][rs], us[c][rs]], axis=1)))
                    sout_ref[seq, h] = (st * jnp.exp(g_last) + kwu[:, GDN_DK:]
                                        - _dot(*_mxu_pair(kwu[:, :GDN_DK], st)))
                wq = _dot(*_mxu_pair(jnp.concatenate([ws[c][rs], qdecs[c][rs]], axis=0), st))
                vn = us[c][rs] - wq[:seg_len]
                v_new[c].append(vn)
                o_inter[c].append(wq[seg_len:])
                if not carry:
                    sout_ref[seq, h] = st * jnp.exp(g_last) + _dot_tn(*_mxu_pair(k_dec, vn))

    for blk, h in chains:
        c = (blk, h)
        r0 = blk * BLOCK
        vn = v_new[c][0] if nseg == 1 else jnp.concatenate(v_new[c], axis=0)
        oi = o_inter[c][0] if nseg == 1 else jnp.concatenate(o_inter[c], axis=0)
        o_h = oi + _dot(aqks[c], vn.astype(BF16))
        cols = slice(h * GDN_DV, (h + 1) * GDN_DV)
        o_ref[r0:r0 + BLOCK, cols] = _rms(o_h) * nw_ref[...] * _silu(z_ref[r0:r0 + BLOCK, cols])


def _gdn(mx, proj, wc, alog, dtb, nw, cin, s0, layer):
    shape = (GDN_HEADS, GDN_DK, GDN_DV)
    body = functools.partial(_gdn_body, seg_len=min(mx.seq_len, BLOCK // 2), carry=mx.carry, nblk=mx.nblk)
    hv = GDN_HEADS * GDN_DV
    return pl.pallas_call(
        body,
        grid=mx.grid,
        in_specs=[mx.rows_spec(GDN_CONV_DIM, 0), mx.rows_spec(hv, C_Z // hv),
                  mx.rows_spec(LANES, C_MISC // LANES),
                  _mixer_const((CONV_W, GDN_CONV_DIM)), _mixer_const((1, LANES)), _mixer_const((1, LANES)),
                  _mixer_const((1, GDN_DV)), mx.conv_spec(layer), mx.state_spec(shape, layer)],
        out_specs=[mx.rows_spec(hv, 0), mx.conv_spec(), mx.state_spec(shape)],
        out_shape=[jax.ShapeDtypeStruct((mx.n_seq * mx.seq_len, hv), F32),
                   jax.ShapeDtypeStruct((mx.n_seq * SUBLANES, GDN_CONV_DIM), F32),
                   jax.ShapeDtypeStruct((mx.n_seq,) + shape, F32)],
        scratch_shapes=[pltpu.VMEM((SUBLANES, GDN_CONV_DIM), F32),
                        pltpu.VMEM((mx.rows, GDN_CONV_DIM), F32)],
        compiler_params=pltpu.CompilerParams(dimension_semantics=("arbitrary", "arbitrary"),
                                             vmem_limit_bytes=VMEM_LIMIT),
        name="gdn_mixer",
    )(proj, proj, proj, wc, alog, dtb, nw, cin, s0)


def _split_bf16(x):
    hi = x.astype(BF16)
    return hi, (x - hi.astype(F32)).astype(BF16)


def _post_body(x_ref, o1_ref, o2_ref, wo_ref, gt_ref, sh_ref, sc_ref, nw_ref, *rest, router):
    if router:
        wr_hi_ref, wr_lo_ref, xo_ref, h_ref, comb_ref = rest
    else:
        xo_ref, h_ref = rest
    half = GLA_HEADS * GLA_DV
    mix = _dot(o1_ref[...].astype(BF16), wo_ref[:half, :]) + _dot(o2_ref[...].astype(BF16), wo_ref[half:, :])
    tm = x_ref.shape[0]
    x = x_ref[...] + _per_token(gt_ref, tm) * mix
    xo_ref[...] = x
    h = _rms(x) * nw_ref[...]
    h = h * (1.0 + _per_token(sc_ref, tm)) + _per_token(sh_ref, tm)
    h_ref[...] = h.astype(h_ref.dtype)
    if router:
        h_hi, h_lo = _split_bf16(h)
        logits = _dot(h_hi, wr_hi_ref[...]) + (_dot(h_hi, wr_lo_ref[...]) + _dot(h_lo, wr_hi_ref[...]))
        lane = lax.broadcasted_iota(jnp.int32, logits.shape, 1)
        lg = jnp.where(lane < N_EXPERTS, logits, -jnp.inf)
        m1 = jnp.max(lg, axis=-1, keepdims=True)
        i1 = jnp.min(jnp.where(lg == m1, lane, LANES), axis=-1, keepdims=True)
        lg2 = jnp.where(lane == i1, -jnp.inf, lg)
        m2 = jnp.max(lg2, axis=-1, keepdims=True)
        i2 = jnp.min(jnp.where(lg2 == m2, lane, LANES), axis=-1, keepdims=True)
        e2 = jnp.exp(m2 - m1)
        w1 = 1.0 / (1.0 + e2)
        comb_ref[...] = (jnp.where(lane == ROUTE_E, i1.astype(F32), 0.0)
                         + jnp.where(lane == ROUTE_E + 1, i2.astype(F32), 0.0)
                         + jnp.where(lane == ROUTE_W, w1, 0.0)
                         + jnp.where(lane == ROUTE_W + 1, e2 * w1, 0.0))


def _post(tok, x, o1, o2, wo, mod, nw, wr=None):
    router = wr is not None
    half = GLA_HEADS * GLA_DV
    in_specs = [tok.row_spec(D_MODEL), tok.row_spec(half), tok.row_spec(half),
                _const_spec((D_MODEL, D_MODEL)), tok.mod_spec(2), tok.mod_spec(3), tok.mod_spec(4),
                _const_spec((1, D_MODEL))]
    args = [x, o1, o2, wo, mod, mod, mod, nw]
    out_specs = [tok.row_spec(D_MODEL), tok.row_spec(D_MODEL)]
    out_shape = [jax.ShapeDtypeStruct((tok.n, D_MODEL), F32),
                 jax.ShapeDtypeStruct((tok.n, D_MODEL), F32 if router else BF16)]
    if router:
        in_specs += [_const_spec((D_MODEL, LANES)), _const_spec((D_MODEL, LANES))]
        args += list(wr)
        out_specs.append(tok.row_spec(LANES))
        out_shape.append(jax.ShapeDtypeStruct((tok.n, LANES), F32))
    return pl.pallas_call(
        functools.partial(_post_body, router=router),
        grid=(tok.tiles,),
        in_specs=in_specs, out_specs=out_specs, out_shape=out_shape,
        compiler_params=pltpu.CompilerParams(dimension_semantics=("arbitrary",),
                                             vmem_limit_bytes=VMEM_LIMIT),
        name="post_mixer",
    )(*args)


def _ffn_body(h_ref, x_ref, gt_ref, wg_ref, wu_ref, wd_ref, o_ref, *, chunk):
    h = h_ref[...]
    d_ff = wg_ref.shape[1]
    acc = None
    for lo in range(0, d_ff, chunk):
        g = _dot(h, wg_ref[:, lo:lo + chunk])
        u = _dot(h, wu_ref[:, lo:lo + chunk])
        part = _dot((_silu(g) * u).astype(BF16), wd_ref[lo:lo + chunk, :])
        acc = part if acc is None else acc + part
    o_ref[...] = x_ref[...] + _per_token(gt_ref, x_ref.shape[0]) * acc


def _ffn(tok, h, x, mod, wg, wu, wd):
    d_ff = wg.shape[1]
    return pl.pallas_call(
        functools.partial(_ffn_body, chunk=d_ff // 2),
        grid=(tok.tiles,),
        in_specs=[tok.row_spec(D_MODEL), tok.row_spec(D_MODEL), tok.mod_spec(5),
                  _const_spec((D_MODEL, d_ff)), _const_spec((D_MODEL, d_ff)), _const_spec((d_ff, D_MODEL))],
        out_specs=tok.row_spec(D_MODEL),
        out_shape=jax.ShapeDtypeStruct((tok.n, D_MODEL), F32),
        compiler_params=pltpu.CompilerParams(dimension_semantics=("arbitrary",),
                                             vmem_limit_bytes=VMEM_LIMIT),
        name="ffn_dense",
    )(h, x, mod, wg, wu, wd)


MOE_TM = 512
MOE_CHUNK = 512


def _route_plan(route, tm):
    n = route.shape[0]
    na = n * TOP_K
    flat_e = route[:, ROUTE_E:ROUTE_E + TOP_K].astype(jnp.int32).T.reshape(na)
    onehot = (flat_e[:, None] == jnp.arange(N_EXPERTS, dtype=jnp.int32)[None, :]).astype(jnp.int32)
    cum = jnp.cumsum(onehot, axis=0)
    rank = jnp.sum(onehot * cum, axis=1) - 1
    counts = cum[-1]
    padded = ((counts + tm - 1) // tm) * tm
    ends = jnp.cumsum(padded)
    starts = ends - padded
    pos = starts[flat_e] + rank
    tiles = -(-na // tm) + N_EXPERTS
    dst = jnp.zeros((tiles * tm,), jnp.int32).at[pos].set(jnp.arange(na, dtype=jnp.int32))
    dst = dst.reshape(tiles, tm)
    tile_start = jnp.arange(tiles, dtype=jnp.int32) * tm
    tile_expert = jnp.sum((ends[None, :] <= tile_start[:, None]).astype(jnp.int32), axis=1)
    tile_expert = jnp.minimum(tile_expert, N_EXPERTS - 1)
    n_valid = (ends[-1] // tm).astype(jnp.int32).reshape(1)
    n_rows = jnp.clip((starts + counts)[tile_expert] - tile_start, 0, tm).astype(jnp.int32)
    real = jnp.arange(tm, dtype=jnp.int32)[None, :] < n_rows[:, None]
    src = jnp.where(real, dst % n, 0)
    return tile_expert, n_valid, n_rows, src.reshape(tiles, 1, tm), dst.reshape(tiles, 1, tm)


def _moe_body(te_ref, nv_ref, nr_ref, src_ref, srcn_ref, dstp_ref, dst_ref, h_hbm, wg_ref, wu_ref, wd_ref, y_hbm,
              hbuf, ybuf, gsem, ssem, *, tm, chunk):
    i = pl.program_id(0)
    last = pl.num_programs(0) - 1
    n_valid = nv_ref[0]
    slot = lax.rem(i, 2)
    other = 1 - slot
    valid = i < n_valid
    next_valid = i + 1 < n_valid
    prev_valid = (i >= 1) & (i - 1 < n_valid)
    n_prev = jnp.where(prev_valid, nr_ref[jnp.maximum(i - 1, 0)], 0)

    def gather_row(idx_ref, s, r):
        return pltpu.make_async_copy(h_hbm.at[pl.ds(idx_ref[0, r], 1)], hbuf.at[s, pl.ds(r, 1)], gsem.at[s])

    def scatter_row(idx_ref, s, r):
        return pltpu.make_async_copy(ybuf.at[s, pl.ds(r, 1)], y_hbm.at[pl.ds(idx_ref[0, r], 1)], ssem.at[s])

    def gather_wait(s):
        pltpu.make_async_copy(h_hbm.at[pl.ds(0, tm)], hbuf.at[s], gsem.at[s]).wait()

    def scatter_wait(s, n):
        n_tiled = pl.multiple_of(lax.shift_left(lax.shift_right_logical(n, 3), 3), SUBLANES)

        @pl.when(n_tiled > 0)
        def _():
            pltpu.make_async_copy(ybuf.at[s, pl.ds(0, n_tiled)], y_hbm.at[pl.ds(0, n_tiled)], ssem.at[s]).wait()

        def body(r, c):
            scatter_row(dst_ref, s, 0).wait()
            return c
        lax.fori_loop(0, n - n_tiled, body, 0)

    def issue_rows(lo, hi):
        for r in range(lo, hi):
            @pl.when(next_valid)
            def _():
                gather_row(srcn_ref, other, r).start()

            @pl.when(r < n_prev)
            def _():
                scatter_row(dstp_ref, other, r).start()

    @pl.when(i == 0)
    def _():
        def body(r, c):
            gather_row(src_ref, 0, r).start()
            return c
        lax.fori_loop(0, tm, body, 0, unroll=8)

    @pl.when((i >= 2) & (i - 2 < n_valid))
    def _():
        scatter_wait(slot, nr_ref[jnp.maximum(i - 2, 0)])

    @pl.when(valid)
    def _():
        gather_wait(slot)
        x = hbuf[slot].astype(BF16)
        d_ff = wg_ref.shape[1]
        n_chunks = d_ff // chunk
        rows_per_chunk = -(-tm // n_chunks)
        acc = None
        for c in range(n_chunks):
            lo = c * chunk
            g = _dot(x, wg_ref[:, lo:lo + chunk])
            u = _dot(x, wu_ref[:, lo:lo + chunk])
            issue_rows(min(c * rows_per_chunk, tm), min((c + 1) * rows_per_chunk, tm))
            part = _dot((_silu(g) * u).astype(BF16), wd_ref[lo:lo + chunk, :])
            acc = part if acc is None else acc + part
        ybuf[slot] = acc

    @pl.when(jnp.logical_not(valid) & prev_valid)
    def _():
        def body(r, c):
            scatter_row(dstp_ref, other, r).start()
            return c
        lax.fori_loop(0, n_prev, body, 0)

    @pl.when(i == last)
    def _():
        @pl.when(valid)
        def _():
            def body(r, c):
                scatter_row(dst_ref, slot, r).start()
                return c
            lax.fori_loop(0, nr_ref[i], body, 0)

        @pl.when(prev_valid)
        def _():
            scatter_wait(other, n_prev)

        @pl.when(valid)
        def _():
            scatter_wait(slot, nr_ref[i])


def _moe(h, route, wg, wu, wd):
    n = h.shape[0]
    tm = MOE_TM
    d_ff = wg.shape[2]
    assert d_ff % MOE_CHUNK == 0
    tile_expert, n_valid, n_rows, src, dst = _route_plan(route, tm)
    tiles = src.shape[0]
    idx_spec = lambda f: pl.BlockSpec((None, 1, tm), f, memory_space=pltpu.SMEM)
    w_spec = lambda shape: pl.BlockSpec((None,) + shape, lambda i, te, nv, nr: (te[i], 0, 0),
                                        pipeline_mode=pl.Buffered(1))
    grid_spec = pltpu.PrefetchScalarGridSpec(
        num_scalar_prefetch=3,
        grid=(tiles,),
        in_specs=[idx_spec(lambda i, te, nv, nr: (i, 0, 0)),
                  idx_spec(lambda i, te, nv, nr: (jnp.minimum(i + 1, tiles - 1), 0, 0)),
                  idx_spec(lambda i, te, nv, nr: (jnp.maximum(i - 1, 0), 0, 0)),
                  idx_spec(lambda i, te, nv, nr: (i, 0, 0)),
                  pl.BlockSpec(memory_space=pl.ANY),
                  w_spec((D_MODEL, d_ff)), w_spec((D_MODEL, d_ff)), w_spec((d_ff, D_MODEL))],
        out_specs=pl.BlockSpec(memory_space=pl.ANY),
        scratch_shapes=[pltpu.VMEM((2, tm, D_MODEL), F32), pltpu.VMEM((2, tm, D_MODEL), F32),
                        pltpu.SemaphoreType.DMA((2,)), pltpu.SemaphoreType.DMA((2,))],
    )
    return pl.pallas_call(
        functools.partial(_moe_body, tm=tm, chunk=MOE_CHUNK),
        grid_spec=grid_spec,
        out_shape=jax.ShapeDtypeStruct((n * TOP_K, D_MODEL), F32),
        compiler_params=pltpu.CompilerParams(dimension_semantics=("arbitrary",),
                                             vmem_limit_bytes=VMEM_LIMIT),
        name="moe_routed",
    )(tile_expert, n_valid, n_rows, src, src, dst, dst, h, wg, wu, wd)


def _combine_body(x_ref, y0_ref, y1_ref, rt_ref, gt_ref, *rest, final):
    rt = rt_ref[...]
    f = rt[:, ROUTE_W:ROUTE_W + 1] * y0_ref[...] + rt[:, ROUTE_W + 1:ROUTE_W + 2] * y1_ref[...]
    tm = x_ref.shape[0]
    x = x_ref[...] + _per_token(gt_ref, tm) * f
    if final:
        sh_ref, sc_ref, nw_ref, o_ref = rest
        x = _rms(x) * nw_ref[...]
        x = x * (1.0 + _per_token(sc_ref, tm)) + _per_token(sh_ref, tm)
    else:
        (o_ref,) = rest
    o_ref[...] = x


def _combine(tok, x, y, route, mod, tok_offset, n_all, final=None):
    b0 = tok_offset // tok.tm
    b1 = (n_all + tok_offset) // tok.tm
    in_specs = [tok.row_spec(D_MODEL),
                pl.BlockSpec((tok.tm, D_MODEL), lambda i: (b0 + i, 0)),
                pl.BlockSpec((tok.tm, D_MODEL), lambda i: (b1 + i, 0)),
                pl.BlockSpec((tok.tm, LANES), lambda i: (b0 + i, 0)),
                tok.mod_spec(5)]
    args = [x, y, y, route, mod]
    if final is not None:
        mod_f, nw = final
        in_specs += [tok.mod_spec(0), tok.mod_spec(1), _const_spec((1, D_MODEL))]
        args += [mod_f, mod_f, nw]
    return pl.pallas_call(
        functools.partial(_combine_body, final=final is not None),
        grid=(tok.tiles,),
        in_specs=in_specs,
        out_specs=tok.row_spec(D_MODEL),
        out_shape=jax.ShapeDtypeStruct((tok.n, D_MODEL), F32),
        compiler_params=pltpu.CompilerParams(dimension_semantics=("arbitrary",),
                                             vmem_limit_bytes=VMEM_LIMIT),
        name="moe_combine",
    )(*args)


def _final_body(x_ref, sh_ref, sc_ref, nw_ref, o_ref):
    tm = x_ref.shape[0]
    h = _rms(x_ref[...]) * nw_ref[...]
    o_ref[...] = h * (1.0 + _per_token(sc_ref, tm)) + _per_token(sh_ref, tm)


def _final(tok, x, mod, nw):
    return pl.pallas_call(
        _final_body,
        grid=(tok.tiles,),
        in_specs=[tok.row_spec(D_MODEL), tok.mod_spec(0), tok.mod_spec(1), _const_spec((1, D_MODEL))],
        out_specs=tok.row_spec(D_MODEL),
        out_shape=jax.ShapeDtypeStruct((tok.n, D_MODEL), F32),
        compiler_params=pltpu.CompilerParams(dimension_semantics=("arbitrary",)),
        name="final_norm",
    )(x, mod, mod, nw)


def _pad_heads(w, heads, dk, kpad):
    lead = w.shape[:-1]
    w = w.reshape(lead + (heads, dk))
    w = jnp.pad(w, [(0, 0)] * len(lead) + [(0, 0), (0, kpad - dk)])
    return w.reshape(lead + (heads * kpad,))


def _layout_w_in(w):
    hk, hv = GLA_HEADS * GLA_DK, GLA_HEADS * GLA_DV
    o = 0
    q1 = w[:, o:o + hk]; o += hk
    k1 = w[:, o:o + hk]; o += hk
    v1 = w[:, o:o + hv]; o += hv
    lr = w[:, o:o + GLA_LOWRANK]; o += GLA_LOWRANK
    g1 = w[:, o:o + hv]; o += hv
    qkv = w[:, o:o + GDN_CONV_DIM]; o += GDN_CONV_DIM
    a = w[:, o:o + GDN_HEADS]; o += GDN_HEADS
    b = w[:, o:o + GDN_HEADS]; o += GDN_HEADS
    z = w[:, o:o + GDN_HEADS * GDN_DV]
    misc = jnp.concatenate([lr, a, b], axis=1)
    misc = jnp.pad(misc, ((0, 0), (0, LANES - misc.shape[1])))
    out = jnp.concatenate([qkv, z, _pad_heads(q1, GLA_HEADS, GLA_DK, GLA_KPAD),
                           _pad_heads(k1, GLA_HEADS, GLA_DK, GLA_KPAD), v1, g1, misc], axis=1)
    return out.astype(BF16)


def _misc_row(v, lane0):
    return jnp.pad(v.astype(F32), (lane0, LANES - lane0 - v.shape[0])).reshape(1, LANES)


class _Trunk:
    def __init__(self, x, mods, mod_f, s_gla, s_gdn, s_conv):
        n_seq, seq_len, _ = x.shape
        self.shape = x.shape
        self.tok = _Tokens(n_seq, seq_len, MOE_TM)
        self.mx = _Mixer(n_seq, seq_len)
        self.x = x.reshape(self.tok.n, D_MODEL)
        self.mods = [self.tok.mod_operand(m) for m in mods]
        self.mod_f = self.tok.mod_operand(mod_f)
        cin = jnp.pad(s_conv, ((0, 0), (0, 0), (SUBLANES - (CONV_W - 1), 0), (0, 0)))
        self.s_gla, self.s_gdn, self.cin = s_gla, s_gdn, cin.reshape(s_conv.shape[0], n_seq * SUBLANES, GDN_CONV_DIM)
        self.new_gla, self.new_gdn, self.new_conv = [], [], []

    def mix(self, l, p):
        n_seq = self.shape[0]
        proj = _inproj(self.tok, self.x, self.mods[l], p["norm_mix"][l], p["w_in"][l])
        o1, sa = _gla(self.mx, proj, p["w_gate"][l], p["b_gate"][l], p["gla_norm"][l], self.s_gla, l)
        o2, cout, sb = _gdn(self.mx, proj, p["w_conv"][l], p["a_log"][l], p["dt_bias"][l], p["gdn_norm"][l],
                            self.cin, self.s_gdn, l)
        self.new_gla.append(sa)
        self.new_gdn.append(sb)
        self.new_conv.append(cout.reshape(n_seq, SUBLANES, GDN_CONV_DIM)[:, SUBLANES - (CONV_W - 1):])
        return o1, o2

    def outputs(self, y):
        return (y.reshape(self.shape), jnp.stack(self.new_gla), jnp.stack(self.new_gdn), jnp.stack(self.new_conv))


def _run(trunks, p, depth):
    n_all = sum(t.tok.n for t in trunks)
    ys = None
    for l in range(depth):
        i = l // 2
        last = l == depth - 1
        posts = []
        for t in trunks:
            o1, o2 = t.mix(l, p)
            wr = None if l % 2 == 0 else p["w_router"][i]
            posts.append(_post(t.tok, t.x, o1, o2, p["w_out"][l], t.mods[l], p["norm_ffn"][l], wr))
        if l % 2 == 0:
            for t, (x, h) in zip(trunks, posts):
                t.x = _ffn(t.tok, h, x, t.mods[l], p["w_ff_gate"][i], p["w_ff_up"][i], p["w_ff_down"][i])
        else:
            h_all = jnp.concatenate([h for _, h, _ in posts], axis=0)
            route = jnp.concatenate([r for _, _, r in posts], axis=0)
            y = _moe(h_all, route, p["w_exp_gate"][i], p["w_exp_up"][i], p["w_exp_down"][i])
            off = 0
            outs = []
            for t, (x, _, _) in zip(trunks, posts):
                fin = (t.mod_f, p["norm_final"]) if last else None
                outs.append(_combine(t.tok, x, y, route, t.mods[l], off, n_all, fin))
                off += t.tok.n
            if last:
                ys = outs
            else:
                for t, x in zip(trunks, outs):
                    t.x = x
    if ys is None:
        ys = [_final(t.tok, t.x, t.mod_f, p["norm_final"]) for t in trunks]
    return ys


def kernel(x_prompt, x_sample, c_prompt, c_sample, state_gla, state_gdn, state_conv, w_ada, b_ada, norm_mix, w_in, w_gla_gate, b_gla_gate, gla_norm, w_conv, a_log, dt_bias, gdn_norm, w_out, norm_ffn, w_ff_gate, w_ff_up, w_ff_down, w_router, w_exp_gate, w_exp_up, w_exp_down, w_ada_final, b_ada_final, norm_final):
    depth = w_in.shape[0]
    bp, bs = x_prompt.shape[0], x_sample.shape[0]
    dt = x_prompt.dtype

    wg_pad = jnp.pad(_pad_heads(w_gla_gate, GLA_HEADS, GLA_DK, GLA_KPAD), ((0, 0), (0, LANES - GLA_LOWRANK), (0, 0)))
    pad_lane = 1.0 - _pad_heads(jnp.ones_like(b_gla_gate), GLA_HEADS, GLA_DK, GLA_KPAD)
    bg_pad = _pad_heads(b_gla_gate, GLA_HEADS, GLA_DK, GLA_KPAD) + 30.0 * pad_lane
    wr = jnp.pad(w_router, ((0, 0), (0, 0), (0, LANES - N_EXPERTS)))
    wr_hi = wr.astype(BF16)
    wr_lo = (wr - wr_hi.astype(F32)).astype(BF16)
    p = {
        "norm_mix": norm_mix.reshape(depth, 1, D_MODEL),
        "w_in": jnp.stack([_layout_w_in(w_in[l]) for l in range(depth)]),
        "w_gate": wg_pad, "b_gate": bg_pad.reshape(depth, 1, -1),
        "gla_norm": gla_norm.reshape(depth, 1, GLA_DV),
        "w_conv": w_conv,
        "a_log": jnp.stack([_misc_row(a_log[l], MISC_A) for l in range(depth)]),
        "dt_bias": jnp.stack([_misc_row(dt_bias[l], MISC_A) for l in range(depth)]),
        "gdn_norm": gdn_norm.reshape(depth, 1, GDN_DV),
        "w_out": w_out.astype(BF16),
        "norm_ffn": norm_ffn.reshape(depth, 1, D_MODEL),
        "w_ff_gate": w_ff_gate.astype(BF16), "w_ff_up": w_ff_up.astype(BF16), "w_ff_down": w_ff_down.astype(BF16),
        "w_router": [(wr_hi[i], wr_lo[i]) for i in range(wr.shape[0])],
        "w_exp_gate": w_exp_gate.astype(BF16), "w_exp_up": w_exp_up.astype(BF16), "w_exp_down": w_exp_down.astype(BF16),
        "norm_final": norm_final.reshape(1, D_MODEL),
    }

    c_all = jnp.concatenate([c_prompt, c_sample], axis=0)
    mods = [_ada(c_all, w_ada[l], b_ada[l]) for l in range(depth)]
    mod_f = _ada(c_all, w_ada_final, b_ada_final)
    z_gla = jnp.zeros((depth, bp, GLA_HEADS, GLA_DK, GLA_DV), dt)
    z_gdn = jnp.zeros((depth, bp, GDN_HEADS, GDN_DK, GDN_DV), dt)
    z_conv = jnp.zeros((depth, bp, CONV_W - 1, GDN_CONV_DIM), dt)
    prompt = _Trunk(x_prompt, [m[:bp] for m in mods], mod_f[:bp], z_gla, z_gdn, z_conv)
    sample = _Trunk(x_sample, [m[bp:] for m in mods], mod_f[bp:], state_gla, state_gdn, state_conv)
    y_p, y_s = _run([prompt, sample], p, depth)
    y_p, gla_p, gdn_p, conv_p = prompt.outputs(y_p)
    y_s, gla_s, gdn_s, conv_s = sample.outputs(y_s)
    return (y_p, y_s, gla_p, gdn_p, conv_p, gla_s, gdn_s, conv_s)
```

```python
import functools
import math

import jax
import jax.numpy as jnp
from jax import lax
from jax.experimental import pallas as pl
from jax.experimental.pallas import tpu as pltpu

F32 = jnp.float32
BF16 = jnp.bfloat16

D_MODEL = 1024
GLA_HEADS = 4
GLA_DK = 64
GLA_DV = 128
GLA_LOWRANK = 16
GLA_GATE_NORMALIZER = 16.0
GDN_HEADS = 4
GDN_DK = 128
GDN_DV = 128
CONV_W = 4
GDN_CONV_DIM = GDN_HEADS * (2 * GDN_DK + GDN_DV)
N_EXPERTS = 8
EPS = 1e-6

LANES = 128
SUBLANES = 8
BLOCK = 128
GLA_KPAD = 128
GLA_BAND = 60.0
VMEM_LIMIT = 56 * 1024 * 1024

C_QKV = 0
C_Z = C_QKV + GDN_CONV_DIM
C_Q1 = C_Z + GDN_HEADS * GDN_DV
C_K1 = C_Q1 + GLA_HEADS * GLA_KPAD
C_V1 = C_K1 + GLA_HEADS * GLA_KPAD
C_G1 = C_V1 + GLA_HEADS * GLA_DV
C_MISC = C_G1 + GLA_HEADS * GLA_DV
PROJ_COLS = C_MISC + LANES
MISC_A = GLA_LOWRANK
MISC_B = GLA_LOWRANK + GDN_HEADS
TOP_K = 2
ROUTE_E = 0
ROUTE_W = TOP_K


def _dot(a, b, precision=None):
    return jnp.dot(a, b, preferred_element_type=F32, precision=precision)


def _dot_nt(a, b):
    return lax.dot_general(a, b, (((1,), (1,)), ((), ())), preferred_element_type=F32)


def _dot_tn(a, b):
    return lax.dot_general(a, b, (((0,), (0,)), ((), ())), preferred_element_type=F32)


def _mxu_pair(a, b):
    a, b = a.astype(BF16), b.astype(BF16)
    if a.shape[0] % (2 * SUBLANES) != 0 or b.shape[0] % (2 * SUBLANES) != 0:
        return a.astype(F32), b.astype(F32)
    return a, b


def _silu(x):
    return x * (1.0 / (1.0 + jnp.exp(-x)))


def _sigmoid(x):
    return 1.0 / (1.0 + jnp.exp(-x))


def _log_sigmoid(x):
    return jnp.minimum(x, 0.0) - jnp.log1p(jnp.exp(-jnp.abs(x)))


def _softplus(x):
    return jnp.maximum(x, 0.0) + jnp.log1p(jnp.exp(-jnp.abs(x)))


def _rms(x):
    return x * lax.rsqrt(jnp.mean(x * x, axis=-1, keepdims=True) + EPS)


def _per_token(mod_ref, tm):
    m = mod_ref[...]
    n_seq = m.shape[0]
    if n_seq == 1:
        return m
    shift = int(math.log2(tm // n_seq))
    assert n_seq << shift == tm
    row = lax.broadcasted_iota(jnp.int32, (tm, n_seq), 0)
    col = lax.broadcasted_iota(jnp.int32, (tm, n_seq), 1)
    return _dot(((row >> shift) == col).astype(F32), m, precision=lax.Precision.HIGHEST)


def _seg_cumsum(x, seg_len):
    pos = lax.broadcasted_iota(jnp.int32, x.shape, 0) & (seg_len - 1)
    s = 1
    while s < seg_len:
        x = x + jnp.where(pos >= s, pltpu.roll(x, s, axis=0), 0.0)
        s *= 2
    return x


def _seg_masks(seg_len):
    row = lax.broadcasted_iota(jnp.int32, (BLOCK, BLOCK), 0)
    col = lax.broadcasted_iota(jnp.int32, (BLOCK, BLOCK), 1)
    shift = int(math.log2(seg_len))
    same = (row >> shift) == (col >> shift)
    return same & (row >= col), same & (row > col), row == col


def _ada_body(c_ref, w_ref, b_ref, o_ref):
    cs = _silu(c_ref[...]).astype(BF16)
    o_ref[...] = _dot(cs, w_ref[...].astype(BF16)) + b_ref[...]


def _ada(c_all, w, b):
    rows = c_all.shape[0]
    nout = w.shape[1]
    tn = min(nout, 1024)
    return pl.pallas_call(
        _ada_body,
        grid=(nout // tn,),
        in_specs=[pl.BlockSpec((rows, D_MODEL), lambda j: (0, 0)),
                  pl.BlockSpec((D_MODEL, tn), lambda j: (0, j)),
                  pl.BlockSpec((1, tn), lambda j: (0, j))],
        out_specs=pl.BlockSpec((rows, tn), lambda j: (0, j)),
        out_shape=jax.ShapeDtypeStruct((rows, nout), F32),
        compiler_params=pltpu.CompilerParams(dimension_semantics=("arbitrary",)),
        name="ada_mod",
    )(c_all, w, b.reshape(1, nout))


class _Tokens:
    def __init__(self, n_seq, seq_len, tm):
        self.n_seq, self.seq_len, self.tm = n_seq, seq_len, tm
        self.n = n_seq * seq_len
        self.per_seq = seq_len % tm == 0
        self.tiles = self.n // tm

    def mod_operand(self, mod):
        if self.per_seq:
            return mod.reshape(self.n_seq, 1, mod.shape[1])
        return mod

    def mod_spec(self, col):
        if self.per_seq:
            tps = self.seq_len // self.tm
            return pl.BlockSpec((None, 1, D_MODEL), lambda i: (i // tps, 0, col))
        return pl.BlockSpec((self.tm // self.seq_len, D_MODEL), lambda i: (i, col))

    def row_spec(self, width, col=0):
        return pl.BlockSpec((self.tm, width), lambda i: (i, col))


def _const_spec(shape):
    nd = len(shape)
    return pl.BlockSpec(shape, lambda i: (0,) * nd, pipeline_mode=pl.Buffered(1))


def _inproj_body(x_ref, sh_ref, sc_ref, nw_ref, w_ref, o_ref):
    tm = x_ref.shape[0]
    h = _rms(x_ref[...]) * nw_ref[...]
    h = h * (1.0 + _per_token(sc_ref, tm)) + _per_token(sh_ref, tm)
    o_ref[...] = _dot(h.astype(BF16), w_ref[...])


def _inproj(tok, x, mod, nw, w):
    return pl.pallas_call(
        _inproj_body,
        grid=(tok.tiles,),
        in_specs=[tok.row_spec(D_MODEL), tok.mod_spec(0), tok.mod_spec(1),
                  _const_spec((1, D_MODEL)), _const_spec((D_MODEL, PROJ_COLS))],
        out_specs=tok.row_spec(PROJ_COLS),
        out_shape=jax.ShapeDtypeStruct((tok.n, PROJ_COLS), F32),
        compiler_params=pltpu.CompilerParams(dimension_semantics=("arbitrary",),
                                             vmem_limit_bytes=VMEM_LIMIT),
        name="in_proj",
    )(x, mod, mod, nw, w)


class _Mixer:
    def __init__(self, n_seq, seq_len):
        self.n_seq, self.seq_len = n_seq, seq_len
        self.carry = seq_len >= BLOCK
        if self.carry:
            self.nblk = 4
            self.rows = self.nblk * BLOCK
            self.grid = (n_seq, seq_len // self.rows)
            self.seqs = 1
        else:
            self.nblk = 1
            self.rows = BLOCK
            self.seqs = BLOCK // seq_len
            self.grid = (n_seq // self.seqs, 1)
        steps = self.grid[1]
        self.row_map = lambda b, t: b * steps + t

    def rows_spec(self, width, col):
        return pl.BlockSpec((self.rows, width), lambda b, t: (self.row_map(b, t), col))

    def state_spec(self, shape, layer=None):
        nd = len(shape)
        if layer is None:
            return pl.BlockSpec((self.seqs,) + shape, lambda b, t: (b,) + (0,) * nd)
        return pl.BlockSpec((None, self.seqs) + shape, lambda b, t: (layer, b) + (0,) * nd)

    def conv_spec(self, layer=None):
        if layer is None:
            return pl.BlockSpec((self.seqs * SUBLANES, GDN_CONV_DIM), lambda b, t: (b, 0))
        return pl.BlockSpec((None, self.seqs * SUBLANES, GDN_CONV_DIM), lambda b, t: (layer, b, 0))


def _mixer_const(shape):
    nd = len(shape)
    return pl.BlockSpec(shape, lambda b, t: (0,) * nd)


def _gla_body(q_ref, k_ref, v_ref, g_ref, misc_ref, wg_ref, bg_ref, nw_ref, s0_ref,
              o_ref, sout_ref, c_scr, a_scr, *, seg_len, carry, nblk):
    nseg = BLOCK // seg_len
    assert nseg == 1 or nseg % SUBLANES == 0
    incl, _, _ = _seg_masks(seg_len)
    heads = range(GLA_HEADS)
    if carry:
        @pl.when(pl.program_id(1) == 0)
        def _():
            sout_ref[...] = s0_ref[...]

    for blk in range(nblk):
        r0 = blk * BLOCK
        pre = _dot(misc_ref[r0:r0 + BLOCK, :], wg_ref[...], precision=lax.Precision.HIGHEST) + bg_ref[...]
        la = _log_sigmoid(pre) * (1.0 / GLA_GATE_NORMALIZER)
        c = _seg_cumsum(la, seg_len)
        n_bands = (jnp.max(-c) * (1.0 / GLA_BAND)).astype(jnp.int32) + 1

        q_dec, k_in, v_in, c_in = [], [], [], []
        for h in heads:
            ks = slice(h * GLA_KPAD, (h + 1) * GLA_KPAD)
            ch = c[:, ks]
            c_scr[h] = ch
            qh = q_ref[r0:r0 + BLOCK, ks] * (GLA_DK ** -0.5)
            kh = k_ref[r0:r0 + BLOCK, ks]
            qd = qh * jnp.exp(ch)
            in0 = ch > -GLA_BAND
            k0 = jnp.where(in0, kh * jnp.exp(jnp.where(in0, -ch, 0.0)), 0.0).astype(BF16)
            a_scr[h] = _dot_nt(qd.astype(BF16), k0)
            q_dec.append(qd)
            k_in.append(kh)
            c_in.append(ch)
            v_in.append(v_ref[r0:r0 + BLOCK, h * GLA_DV:(h + 1) * GLA_DV])

        def band_body(r, carry_):
            lo = r.astype(F32) * GLA_BAND
            for h in heads:
                ks = slice(h * GLA_KPAD, (h + 1) * GLA_KPAD)
                ch = c_scr[h]
                qh = q_ref[r0:r0 + BLOCK, ks] * (GLA_DK ** -0.5)
                kh = k_ref[r0:r0 + BLOCK, ks]
                qb = (qh * jnp.exp(jnp.minimum(ch + lo, 0.0))).astype(BF16)
                inb = (ch <= -lo) & (ch > -(lo + GLA_BAND))
                kb = jnp.where(inb, kh * jnp.exp(jnp.where(inb, -ch - lo, 0.0)), 0.0).astype(BF16)
                a_scr[h] = a_scr[h] + _dot_nt(qb, kb)
            return carry_

        lax.fori_loop(1, n_bands, band_body, 0)

        decay_cols = []
        for h in heads:
            c_last = c_scr[h, pl.ds(seg_len - 1, nseg, stride=seg_len), :]
            if nseg == 1:
                c_last = jnp.broadcast_to(c_last, (SUBLANES, GLA_KPAD))
            c_last = jnp.concatenate([c_last, jnp.zeros((BLOCK - c_last.shape[0], GLA_KPAD), F32)], axis=0)
            decay_cols.append(jnp.exp(c_last.T[:GLA_DK, :]))
        o_intra = [_dot(jnp.where(incl, a_scr[h], 0.0).astype(BF16), v_in[h].astype(BF16)) for h in heads]
        o_inter = [[] for _ in heads]
        for sg in range(nseg):
            rs = slice(sg * seg_len, (sg + 1) * seg_len)
            seq = 0 if carry else blk * nseg + sg
            for h in heads:
                c_last = c_in[h][(sg + 1) * seg_len - 1:(sg + 1) * seg_len, :]
                st = sout_ref[seq, h] if carry else s0_ref[seq, h]
                o_inter[h].append(_dot(*_mxu_pair(q_dec[h][rs][:, :GLA_DK], st)))
                k_dec = (k_in[h][rs] * jnp.exp(c_last - c_in[h][rs]))[:, :GLA_DK]
                sout_ref[seq, h] = st * decay_cols[h][:, sg:sg + 1] + _dot_tn(*_mxu_pair(k_dec, v_in[h][rs]))
        for h in heads:
            vs = slice(h * GLA_DV, (h + 1) * GLA_DV)
            o_h = o_intra[h] + (o_inter[h][0] if nseg == 1 else jnp.concatenate(o_inter[h], axis=0))
            o_ref[r0:r0 + BLOCK, vs] = _rms(o_h) * nw_ref[...] * _silu(g_ref[r0:r0 + BLOCK, vs])


def _gla(mx, proj, wg, bg, nw, s0, layer):
    shape = (GLA_HEADS, GLA_DK, GLA_DV)
    body = functools.partial(_gla_body, seg_len=min(mx.seq_len, BLOCK), carry=mx.carry, nblk=mx.nblk)
    hk = GLA_HEADS * GLA_KPAD
    hv = GLA_HEADS * GLA_DV
    return pl.pallas_call(
        body,
        grid=mx.grid,
        in_specs=[mx.rows_spec(hk, C_Q1 // hk), mx.rows_spec(hk, C_K1 // hk),
                  mx.rows_spec(hv, C_V1 // hv), mx.rows_spec(hv, C_G1 // hv),
                  mx.rows_spec(LANES, C_MISC // LANES),
                  _mixer_const((LANES, hk)), _mixer_const((1, hk)), _mixer_const((1, GLA_DV)),
                  mx.state_spec(shape, layer)],
        out_specs=[mx.rows_spec(hv, 0), mx.state_spec(shape)],
        out_shape=[jax.ShapeDtypeStruct((mx.n_seq * mx.seq_len, hv), F32),
                   jax.ShapeDtypeStruct((mx.n_seq,) + shape, F32)],
        scratch_shapes=[pltpu.VMEM((GLA_HEADS, BLOCK, GLA_KPAD), F32), pltpu.VMEM((GLA_HEADS, BLOCK, BLOCK), F32)],
        compiler_params=pltpu.CompilerParams(dimension_semantics=("arbitrary", "arbitrary"),
                                             vmem_limit_bytes=VMEM_LIMIT),
        name="gla_mixer",
    )(proj, proj, proj, proj, proj, wg, bg, nw, s0)


def _gdn_body(x_ref, z_ref, misc_ref, wc_ref, alog_ref, dtb_ref, nw_ref, cin_ref, s0_ref,
              o_ref, cout_ref, sout_ref, prev_scr, act_scr, *, seg_len, carry, nblk):
    nseg = BLOCK // seg_len
    rows = nblk * BLOCK
    incl, strict, eye = _seg_masks(seg_len)
    eye_f = eye.astype(F32)
    row = lax.broadcasted_iota(jnp.int32, (BLOCK, BLOCK), 0)
    col = lax.broadcasted_iota(jnp.int32, (BLOCK, BLOCK), 1)
    off_masks = [((row >> (s + 1)) == (col >> (s + 1))) & ((row >> s) != (col >> s)) & (row > col)
                 for s in range(int(math.log2(seg_len)))]
    if carry:
        @pl.when(pl.program_id(1) == 0)
        def _():
            sout_ref[...] = s0_ref[...]
            prev_scr[...] = cin_ref[...]

    x = x_ref[...]
    conv = x * wc_ref[CONV_W - 1:CONV_W, :]
    for d in range(1, CONV_W):
        tap = pltpu.roll(x, d, axis=0)
        if carry:
            pos = lax.broadcasted_iota(jnp.int32, (SUBLANES, GDN_CONV_DIM), 0)
            head = jnp.where(pos >= d, tap[:SUBLANES], pltpu.roll(prev_scr[...], d, axis=0))
            tap = jnp.concatenate([head, tap[SUBLANES:]], axis=0)
        else:
            pos = lax.broadcasted_iota(jnp.int32, x.shape, 0) & (SUBLANES - 1)
            tap = jnp.where(pos >= d, tap, pltpu.roll(cin_ref[...], rows - SUBLANES + d, axis=0))
        conv = conv + tap * wc_ref[CONV_W - 1 - d:CONV_W - d, :]
    act_scr[...] = _silu(conv)
    if carry:
        prev_scr[...] = x[rows - SUBLANES:]
        cout_ref[...] = x[rows - SUBLANES:]
    else:
        cout_ref[...] = x

    chains = [(blk, h) for blk in range(nblk) for h in range(GDN_HEADS)]
    qs, ks, vs_, gcols, betas, egams, ms, aqks = {}, {}, {}, {}, {}, {}, {}, {}
    for blk in range(nblk):
        r0 = blk * BLOCK
        misc = misc_ref[r0:r0 + BLOCK, :]
        g = -jnp.exp(alog_ref[...]) * _softplus(misc + dtb_ref[...])
        beta_all = _sigmoid(misc)
        gam = _seg_cumsum(g, seg_len)
        gam_t = gam.T
        for h in range(GDN_HEADS):
            c = (blk, h)
            qh = act_scr[r0:r0 + BLOCK, h * GDN_DK:(h + 1) * GDN_DK]
            kh = act_scr[r0:r0 + BLOCK, (GDN_HEADS + h) * GDN_DK:(GDN_HEADS + h + 1) * GDN_DK]
            vs_[c] = act_scr[r0:r0 + BLOCK, 2 * GDN_HEADS * GDN_DK + h * GDN_DV:2 * GDN_HEADS * GDN_DK + (h + 1) * GDN_DV]
            qs[c] = qh * lax.rsqrt(jnp.sum(qh * qh, axis=-1, keepdims=True) + EPS) * (GDN_DK ** -0.5)
            ks[c] = kh * lax.rsqrt(jnp.sum(kh * kh, axis=-1, keepdims=True) + EPS)
            gcols[c] = gam[:, MISC_A + h:MISC_A + h + 1]
            grow = gam_t[MISC_A + h:MISC_A + h + 1, :]
            betas[c] = beta_all[:, MISC_B + h:MISC_B + h + 1]
            egams[c] = jnp.exp(gcols[c])
            kb = ks[c].astype(BF16)
            kq = _dot_nt(jnp.concatenate([kb, qs[c].astype(BF16)], axis=0), kb)
            lmask = jnp.exp(jnp.where(incl, gcols[c] - grow, -1e30))
            ms[c] = jnp.where(strict, betas[c] * kq[:BLOCK] * lmask, 0.0)
            aqks[c] = (kq[BLOCK:] * lmask).astype(BF16)

    invs = {c: eye_f - jnp.where(off_masks[0], ms[c], 0.0) for c in chains}
    for off in off_masks[1:]:
        tbs = {c: invs[c].astype(BF16) for c in chains}
        tms = {c: _dot(tbs[c], jnp.where(off, ms[c], 0.0).astype(BF16)).astype(BF16) for c in chains}
        invs = {c: invs[c] - _dot(tms[c], tbs[c]) for c in chains}
    us, ws, qdecs = {}, {}, {}
    for c in chains:
        rhs = jnp.concatenate([vs_[c] * betas[c], ks[c] * (betas[c] * egams[c])], axis=1).astype(BF16)
        uw = _dot(invs[c].astype(BF16), rhs)
        us[c], ws[c] = uw[:, :GDN_DV], uw[:, GDN_DV:]
        qdecs[c] = qs[c] * egams[c]

    v_new = {c: [] for c in chains}
    o_inter = {c: [] for c in chains}
    for blk in range(nblk):
        for sg in range(nseg):
            rs = slice(sg * seg_len, (sg + 1) * seg_len)
            seq = 0 if carry else blk * nseg + sg
            for h in range(GDN_HEADS):
                c = (blk, h)
                g_last = gcols[c][(sg + 1) * seg_len - 1:(sg + 1) * seg_len, :]
                st = sout_ref[seq, h] if carry else s0_ref[seq, h]
                k_dec = ks[c][rs] * jnp.exp(g_last - gcols[c][rs])
                if carry:
                    kwu = _dot_tn(*_mxu_pair(k_dec, jnp.concatenate([ws[c][rs], us[c][rs]], axis=1)))
                    sout_ref[seq, h] = (st * jnp.exp(g_last) + kwu[:, GDN_DK:]
                                        - _dot(*_mxu_pair(kwu[:, :GDN_DK], st)))
                wq = _dot(*_mxu_pair(jnp.concatenate([ws[c][rs], qdecs[c][rs]], axis=0), st))
                vn = us[c][rs] - wq[:seg_len]
                v_new[c].append(vn)
                o_inter[c].append(wq[seg_len:])
                if not carry:
                    sout_ref[seq, h] = st * jnp.exp(g_last) + _dot_tn(*_mxu_pair(k_dec, vn))

    for blk, h in chains:
        c = (blk, h)
        r0 = blk * BLOCK
        vn = v_new[c][0] if nseg == 1 else jnp.concatenate(v_new[c], axis=0)
        oi = o_inter[c][0] if nseg == 1 else jnp.concatenate(o_inter[c], axis=0)
        o_h = oi + _dot(aqks[c], vn.astype(BF16))
        cols = slice(h * GDN_DV, (h + 1) * GDN_DV)
        o_ref[r0:r0 + BLOCK, cols] = _rms(o_h) * nw_ref[...] * _silu(z_ref[r0:r0 + BLOCK, cols])


def _gdn(mx, proj, wc, alog, dtb, nw, cin, s0, layer):
    shape = (GDN_HEADS, GDN_DK, GDN_DV)
    body = functools.partial(_gdn_body, seg_len=min(mx.seq_len, BLOCK // 2), carry=mx.carry, nblk=mx.nblk)
    hv = GDN_HEADS * GDN_DV
    return pl.pallas_call(
        body,
        grid=mx.grid,
        in_specs=[mx.rows_spec(GDN_CONV_DIM, 0), mx.rows_spec(hv, C_Z // hv),
                  mx.rows_spec(LANES, C_MISC // LANES),
                  _mixer_const((CONV_W, GDN_CONV_DIM)), _mixer_const((1, LANES)), _mixer_const((1, LANES)),
                  _mixer_const((1, GDN_DV)), mx.conv_spec(layer), mx.state_spec(shape, layer)],
        out_specs=[mx.rows_spec(hv, 0), mx.conv_spec(), mx.state_spec(shape)],
        out_shape=[jax.ShapeDtypeStruct((mx.n_seq * mx.seq_len, hv), F32),
                   jax.ShapeDtypeStruct((mx.n_seq * SUBLANES, GDN_CONV_DIM), F32),
                   jax.ShapeDtypeStruct((mx.n_seq,) + shape, F32)],
        scratch_shapes=[pltpu.VMEM((SUBLANES, GDN_CONV_DIM), F32),
                        pltpu.VMEM((mx.rows, GDN_CONV_DIM), F32)],
        compiler_params=pltpu.CompilerParams(dimension_semantics=("arbitrary", "arbitrary"),
                                             vmem_limit_bytes=VMEM_LIMIT),
        name="gdn_mixer",
    )(proj, proj, proj, wc, alog, dtb, nw, cin, s0)


def _split_bf16(x):
    hi = x.astype(BF16)
    return hi, (x - hi.astype(F32)).astype(BF16)


def _post_body(x_ref, o1_ref, o2_ref, wo_ref, gt_ref, sh_ref, sc_ref, nw_ref, *rest, router):
    if router:
        wr_hi_ref, wr_lo_ref, xo_ref, h_ref, comb_ref = rest
    else:
        xo_ref, h_ref = rest
    half = GLA_HEADS * GLA_DV
    mix = _dot(o1_ref[...].astype(BF16), wo_ref[:half, :]) + _dot(o2_ref[...].astype(BF16), wo_ref[half:, :])
    tm = x_ref.shape[0]
    x = x_ref[...] + _per_token(gt_ref, tm) * mix
    xo_ref[...] = x
    h = _rms(x) * nw_ref[...]
    h = h * (1.0 + _per_token(sc_ref, tm)) + _per_token(sh_ref, tm)
    h_ref[...] = h.astype(h_ref.dtype)
    if router:
        h_hi, h_lo = _split_bf16(h)
        logits = _dot(h_hi, wr_hi_ref[...]) + (_dot(h_hi, wr_lo_ref[...]) + _dot(h_lo, wr_hi_ref[...]))
        lane = lax.broadcasted_iota(jnp.int32, logits.shape, 1)
        lg = jnp.where(lane < N_EXPERTS, logits, -jnp.inf)
        m1 = jnp.max(lg, axis=-1, keepdims=True)
        i1 = jnp.min(jnp.where(lg == m1, lane, LANES), axis=-1, keepdims=True)
        lg2 = jnp.where(lane == i1, -jnp.inf, lg)
        m2 = jnp.max(lg2, axis=-1, keepdims=True)
        i2 = jnp.min(jnp.where(lg2 == m2, lane, LANES), axis=-1, keepdims=True)
        e2 = jnp.exp(m2 - m1)
        w1 = 1.0 / (1.0 + e2)
        comb_ref[...] = (jnp.where(lane == ROUTE_E, i1.astype(F32), 0.0)
                         + jnp.where(lane == ROUTE_E + 1, i2.astype(F32), 0.0)
                         + jnp.where(lane == ROUTE_W, w1, 0.0)
                         + jnp.where(lane == ROUTE_W + 1, e2 * w1, 0.0))


def _post(tok, x, o1, o2, wo, mod, nw, wr=None):
    router = wr is not None
    half = GLA_HEADS * GLA_DV
    in_specs = [tok.row_spec(D_MODEL), tok.row_spec(half), tok.row_spec(half),
                _const_spec((D_MODEL, D_MODEL)), tok.mod_spec(2), tok.mod_spec(3), tok.mod_spec(4),
                _const_spec((1, D_MODEL))]
    args = [x, o1, o2, wo, mod, mod, mod, nw]
    out_specs = [tok.row_spec(D_MODEL), tok.row_spec(D_MODEL)]
    out_shape = [jax.ShapeDtypeStruct((tok.n, D_MODEL), F32),
                 jax.ShapeDtypeStruct((tok.n, D_MODEL), F32 if router else BF16)]
    if router:
        in_specs += [_const_spec((D_MODEL, LANES)), _const_spec((D_MODEL, LANES))]
        args += list(wr)
        out_specs.append(tok.row_spec(LANES))
        out_shape.append(jax.ShapeDtypeStruct((tok.n, LANES), F32))
    return pl.pallas_call(
        functools.partial(_post_body, router=router),
        grid=(tok.tiles,),
        in_specs=in_specs, out_specs=out_specs, out_shape=out_shape,
        compiler_params=pltpu.CompilerParams(dimension_semantics=("arbitrary",),
                                             vmem_limit_bytes=VMEM_LIMIT),
        name="post_mixer",
    )(*args)


def _ffn_body(h_ref, x_ref, gt_ref, wg_ref, wu_ref, wd_ref, o_ref, *, chunk):
    h = h_ref[...]
    d_ff = wg_ref.shape[1]
    acc = None
    for lo in range(0, d_ff, chunk):
        g = _dot(h, wg_ref[:, lo:lo + chunk])
        u = _dot(h, wu_ref[:, lo:lo + chunk])
        part = _dot((_silu(g) * u).astype(BF16), wd_ref[lo:lo + chunk, :])
        acc = part if acc is None else acc + part
    o_ref[...] = x_ref[...] + _per_token(gt_ref, x_ref.shape[0]) * acc


def _ffn(tok, h, x, mod, wg, wu, wd):
    d_ff = wg.shape[1]
    return pl.pallas_call(
        functools.partial(_ffn_body, chunk=d_ff // 2),
        grid=(tok.tiles,),
        in_specs=[tok.row_spec(D_MODEL), tok.row_spec(D_MODEL), tok.mod_spec(5),
                  _const_spec((D_MODEL, d_ff)), _const_spec((D_MODEL, d_ff)), _const_spec((d_ff, D_MODEL))],
        out_specs=tok.row_spec(D_MODEL),
        out_shape=jax.ShapeDtypeStruct((tok.n, D_MODEL), F32),
        compiler_params=pltpu.CompilerParams(dimension_semantics=("arbitrary",),
                                             vmem_limit_bytes=VMEM_LIMIT),
        name="ffn_dense",
    )(h, x, mod, wg, wu, wd)


MOE_TM = 512
MOE_CHUNK = 512


def _route_plan(route, tm):
    n = route.shape[0]
    na = n * TOP_K
    flat_e = route[:, ROUTE_E:ROUTE_E + TOP_K].astype(jnp.int32).T.reshape(na)
    onehot = (flat_e[:, None] == jnp.arange(N_EXPERTS, dtype=jnp.int32)[None, :]).astype(jnp.int32)
    cum = jnp.cumsum(onehot, axis=0)
    rank = jnp.sum(onehot * cum, axis=1) - 1
    counts = cum[-1]
    padded = ((counts + tm - 1) // tm) * tm
    ends = jnp.cumsum(padded)
    starts = ends - padded
    pos = starts[flat_e] + rank
    tiles = -(-na // tm) + N_EXPERTS
    dst = jnp.zeros((tiles * tm,), jnp.int32).at[pos].set(jnp.arange(na, dtype=jnp.int32))
    dst = dst.reshape(tiles, tm)
    tile_start = jnp.arange(tiles, dtype=jnp.int32) * tm
    tile_expert = jnp.sum((ends[None, :] <= tile_start[:, None]).astype(jnp.int32), axis=1)
    tile_expert = jnp.minimum(tile_expert, N_EXPERTS - 1)
    n_valid = (ends[-1] // tm).astype(jnp.int32).reshape(1)
    n_rows = jnp.clip((starts + counts)[tile_expert] - tile_start, 0, tm).astype(jnp.int32)
    real = jnp.arange(tm, dtype=jnp.int32)[None, :] < n_rows[:, None]
    src = jnp.where(real, dst % n, 0)
    return tile_expert, n_valid, n_rows, src.reshape(tiles, 1, tm), dst.reshape(tiles, 1, tm)


def _moe_body(te_ref, nv_ref, nr_ref, src_ref, srcn_ref, dstp_ref, dst_ref, h_hbm, wg_hbm, wu_hbm, wd_hbm, y_hbm,
              hbuf, ybuf, wg_res, wu_res, wd_res, stage, stage_d, gsem, ssem, wsem, *, tm, chunk):
    i = pl.program_id(0)
    last = pl.num_programs(0) - 1
    n_valid = nv_ref[0]
    slot = lax.rem(i, 2)
    other = 1 - slot
    valid = i < n_valid
    next_valid = i + 1 < n_valid
    prev_valid = (i >= 1) & (i - 1 < n_valid)
    n_prev = jnp.where(prev_valid, nr_ref[jnp.maximum(i - 1, 0)], 0)
    expert = te_ref[i]
    new_expert = (i == 0) | (expert != te_ref[jnp.maximum(i - 1, 0)])
    d_ff = wg_res.shape[1]
    n_chunks = d_ff // chunk

    def gather_row(idx_ref, s, r):
        return pltpu.make_async_copy(h_hbm.at[pl.ds(idx_ref[0, r], 1)], hbuf.at[s, pl.ds(r, 1)], gsem.at[s])

    def scatter_row(idx_ref, s, r):
        return pltpu.make_async_copy(ybuf.at[s, pl.ds(r, 1)], y_hbm.at[pl.ds(idx_ref[0, r], 1)], ssem.at[s])

    def weight_copies(c):
        ws, cols = c % 2, pl.ds(c * chunk, chunk)
        return (pltpu.make_async_copy(wg_hbm.at[expert, :, cols], stage.at[ws, 0], wsem.at[ws]),
                pltpu.make_async_copy(wu_hbm.at[expert, :, cols], stage.at[ws, 1], wsem.at[ws]),
                pltpu.make_async_copy(wd_hbm.at[expert, cols, :], stage_d.at[ws], wsem.at[ws]))

    def gather_wait(s):
        pltpu.make_async_copy(h_hbm.at[pl.ds(0, tm)], hbuf.at[s], gsem.at[s]).wait()

    def scatter_wait(s, n):
        n_tiled = pl.multiple_of(lax.shift_left(lax.shift_right_logical(n, 3), 3), SUBLANES)

        @pl.when(n_tiled > 0)
        def _():
            pltpu.make_async_copy(ybuf.at[s, pl.ds(0, n_tiled)], y_hbm.at[pl.ds(0, n_tiled)], ssem.at[s]).wait()

        def body(r, c):
            scatter_row(dst_ref, s, 0).wait()
            return c
        lax.fori_loop(0, n - n_tiled, body, 0)

    def issue_rows(lo, hi):
        for r in range(lo, hi):
            @pl.when(next_valid)
            def _():
                gather_row(srcn_ref, other, r).start()

            @pl.when(r < n_prev)
            def _():
                scatter_row(dstp_ref, other, r).start()

    def compute_tile(load_weights):
        gather_wait(slot)
        x = hbuf[slot].astype(BF16)
        rows_per_chunk = -(-tm // n_chunks)
        if load_weights:
            for c in range(min(2, n_chunks)):
                for cp in weight_copies(c):
                    cp.start()
        acc = None
        for c in range(n_chunks):
            lo = c * chunk
            if load_weights:
                for cp in weight_copies(c):
                    cp.wait()
                ws = c % 2
                wg_res[:, lo:lo + chunk] = stage[ws, 0].astype(BF16)
                wu_res[:, lo:lo + chunk] = stage[ws, 1].astype(BF16)
                wd_res[lo:lo + chunk, :] = stage_d[ws].astype(BF16)
                if c + 2 < n_chunks:
                    for cp in weight_copies(c + 2):
                        cp.start()
            g = _dot(x, wg_res[:, lo:lo + chunk])
            u = _dot(x, wu_res[:, lo:lo + chunk])
            issue_rows(min(c * rows_per_chunk, tm), min((c + 1) * rows_per_chunk, tm))
            part = _dot((_silu(g) * u).astype(BF16), wd_res[lo:lo + chunk, :])
            acc = part if acc is None else acc + part
        ybuf[slot] = acc

    @pl.when(i == 0)
    def _():
        def body(r, c):
            gather_row(src_ref, 0, r).start()
            return c
        lax.fori_loop(0, tm, body, 0, unroll=8)

    @pl.when((i >= 2) & (i - 2 < n_valid))
    def _():
        scatter_wait(slot, nr_ref[jnp.maximum(i - 2, 0)])

    @pl.when(valid & new_expert)
    def _():
        compute_tile(True)

    @pl.when(valid & jnp.logical_not(new_expert))
    def _():
        compute_tile(False)

    @pl.when(jnp.logical_not(valid) & prev_valid)
    def _():
        def body(r, c):
            scatter_row(dstp_ref, other, r).start()
            return c
        lax.fori_loop(0, n_prev, body, 0)

    @pl.when(i == last)
    def _():
        @pl.when(valid)
        def _():
            def body(r, c):
                scatter_row(dst_ref, slot, r).start()
                return c
            lax.fori_loop(0, nr_ref[i], body, 0)

        @pl.when(prev_valid)
        def _():
            scatter_wait(other, n_prev)

        @pl.when(valid)
        def _():
            scatter_wait(slot, nr_ref[i])


def _moe(h, route, wg, wu, wd):
    n = h.shape[0]
    tm = MOE_TM
    d_ff = wg.shape[2]
    assert d_ff % MOE_CHUNK == 0
    tile_expert, n_valid, n_rows, src, dst = _route_plan(route, tm)
    tiles = src.shape[0]
    idx_spec = lambda f: pl.BlockSpec((None, 1, tm), f, memory_space=pltpu.SMEM)
    any_spec = pl.BlockSpec(memory_space=pl.ANY)
    grid_spec = pltpu.PrefetchScalarGridSpec(
        num_scalar_prefetch=3,
        grid=(tiles,),
        in_specs=[idx_spec(lambda i, te, nv, nr: (i, 0, 0)),
                  idx_spec(lambda i, te, nv, nr: (jnp.minimum(i + 1, tiles - 1), 0, 0)),
                  idx_spec(lambda i, te, nv, nr: (jnp.maximum(i - 1, 0), 0, 0)),
                  idx_spec(lambda i, te, nv, nr: (i, 0, 0)),
                  any_spec, any_spec, any_spec, any_spec],
        out_specs=any_spec,
        scratch_shapes=[pltpu.VMEM((2, tm, D_MODEL), F32), pltpu.VMEM((2, tm, D_MODEL), F32),
                        pltpu.VMEM((D_MODEL, d_ff), BF16), pltpu.VMEM((D_MODEL, d_ff), BF16),
                        pltpu.VMEM((d_ff, D_MODEL), BF16),
                        pltpu.VMEM((2, 2, D_MODEL, MOE_CHUNK), F32), pltpu.VMEM((2, MOE_CHUNK, D_MODEL), F32),
                        pltpu.SemaphoreType.DMA((2,)), pltpu.SemaphoreType.DMA((2,)),
                        pltpu.SemaphoreType.DMA((2,))],
    )
    return pl.pallas_call(
        functools.partial(_moe_body, tm=tm, chunk=MOE_CHUNK),
        grid_spec=grid_spec,
        out_shape=jax.ShapeDtypeStruct((n * TOP_K, D_MODEL), F32),
        compiler_params=pltpu.CompilerParams(dimension_semantics=("arbitrary",),
                                             vmem_limit_bytes=VMEM_LIMIT),
        name="moe_routed",
    )(tile_expert, n_valid, n_rows, src, src, dst, dst, h, wg, wu, wd)


def _combine_body(x_ref, y0_ref, y1_ref, rt_ref, gt_ref, *rest, final):
    rt = rt_ref[...]
    f = rt[:, ROUTE_W:ROUTE_W + 1] * y0_ref[...] + rt[:, ROUTE_W + 1:ROUTE_W + 2] * y1_ref[...]
    tm = x_ref.shape[0]
    x = x_ref[...] + _per_token(gt_ref, tm) * f
    if final:
        sh_ref, sc_ref, nw_ref, o_ref = rest
        x = _rms(x) * nw_ref[...]
        x = x * (1.0 + _per_token(sc_ref, tm)) + _per_token(sh_ref, tm)
    else:
        (o_ref,) = rest
    o_ref[...] = x


def _combine(tok, x, y, route, mod, tok_offset, n_all, final=None):
    b0 = tok_offset // tok.tm
    b1 = (n_all + tok_offset) // tok.tm
    in_specs = [tok.row_spec(D_MODEL),
                pl.BlockSpec((tok.tm, D_MODEL), lambda i: (b0 + i, 0)),
                pl.BlockSpec((tok.tm, D_MODEL), lambda i: (b1 + i, 0)),
                pl.BlockSpec((tok.tm, LANES), lambda i: (b0 + i, 0)),
                tok.mod_spec(5)]
    args = [x, y, y, route, mod]
    if final is not None:
        mod_f, nw = final
        in_specs += [tok.mod_spec(0), tok.mod_spec(1), _const_spec((1, D_MODEL))]
        args += [mod_f, mod_f, nw]
    return pl.pallas_call(
        functools.partial(_combine_body, final=final is not None),
        grid=(tok.tiles,),
        in_specs=in_specs,
        out_specs=tok.row_spec(D_MODEL),
        out_shape=jax.ShapeDtypeStruct((tok.n, D_MODEL), F32),
        compiler_params=pltpu.CompilerParams(dimension_semantics=("arbitrary",),
                                             vmem_limit_bytes=VMEM_LIMIT),
        name="moe_combine",
    )(*args)


def _final_body(x_ref, sh_ref, sc_ref, nw_ref, o_ref):
    tm = x_ref.shape[0]
    h = _rms(x_ref[...]) * nw_ref[...]
    o_ref[...] = h * (1.0 + _per_token(sc_ref, tm)) + _per_token(sh_ref, tm)


def _final(tok, x, mod, nw):
    return pl.pallas_call(
        _final_body,
        grid=(tok.tiles,),
        in_specs=[tok.row_spec(D_MODEL), tok.mod_spec(0), tok.mod_spec(1), _const_spec((1, D_MODEL))],
        out_specs=tok.row_spec(D_MODEL),
        out_shape=jax.ShapeDtypeStruct((tok.n, D_MODEL), F32),
        compiler_params=pltpu.CompilerParams(dimension_semantics=("arbitrary",)),
        name="final_norm",
    )(x, mod, mod, nw)


def _pad_heads(w, heads, dk, kpad):
    lead = w.shape[:-1]
    w = w.reshape(lead + (heads, dk))
    w = jnp.pad(w, [(0, 0)] * len(lead) + [(0, 0), (0, kpad - dk)])
    return w.reshape(lead + (heads * kpad,))


def _layout_w_in(w):
    hk, hv = GLA_HEADS * GLA_DK, GLA_HEADS * GLA_DV
    o = 0
    q1 = w[:, o:o + hk]; o += hk
    k1 = w[:, o:o + hk]; o += hk
    v1 = w[:, o:o + hv]; o += hv
    lr = w[:, o:o + GLA_LOWRANK]; o += GLA_LOWRANK
    g1 = w[:, o:o + hv]; o += hv
    qkv = w[:, o:o + GDN_CONV_DIM]; o += GDN_CONV_DIM
    a = w[:, o:o + GDN_HEADS]; o += GDN_HEADS
    b = w[:, o:o + GDN_HEADS]; o += GDN_HEADS
    z = w[:, o:o + GDN_HEADS * GDN_DV]
    misc = jnp.concatenate([lr, a, b], axis=1)
    misc = jnp.pad(misc, ((0, 0), (0, LANES - misc.shape[1])))
    out = jnp.concatenate([qkv, z, _pad_heads(q1, GLA_HEADS, GLA_DK, GLA_KPAD),
                           _pad_heads(k1, GLA_HEADS, GLA_DK, GLA_KPAD), v1, g1, misc], axis=1)
    return out.astype(BF16)


def _misc_row(v, lane0):
    return jnp.pad(v.astype(F32), (lane0, LANES - lane0 - v.shape[0])).reshape(1, LANES)


class _Trunk:
    def __init__(self, x, mods, mod_f, s_gla, s_gdn, s_conv):
        n_seq, seq_len, _ = x.shape
        self.shape = x.shape
        self.tok = _Tokens(n_seq, seq_len, MOE_TM)
        self.mx = _Mixer(n_seq, seq_len)
        self.x = x.reshape(self.tok.n, D_MODEL)
        self.mods = [self.tok.mod_operand(m) for m in mods]
        self.mod_f = self.tok.mod_operand(mod_f)
        cin = jnp.pad(s_conv, ((0, 0), (0, 0), (SUBLANES - (CONV_W - 1), 0), (0, 0)))
        self.s_gla, self.s_gdn, self.cin = s_gla, s_gdn, cin.reshape(s_conv.shape[0], n_seq * SUBLANES, GDN_CONV_DIM)
        self.new_gla, self.new_gdn, self.new_conv = [], [], []

    def mix(self, l, p):
        n_seq = self.shape[0]
        proj = _inproj(self.tok, self.x, self.mods[l], p["norm_mix"][l], p["w_in"][l])
        o1, sa = _gla(self.mx, proj, p["w_gate"][l], p["b_gate"][l], p["gla_norm"][l], self.s_gla, l)
        o2, cout, sb = _gdn(self.mx, proj, p["w_conv"][l], p["a_log"][l], p["dt_bias"][l], p["gdn_norm"][l],
                            self.cin, self.s_gdn, l)
        self.new_gla.append(sa)
        self.new_gdn.append(sb)
        self.new_conv.append(cout.reshape(n_seq, SUBLANES, GDN_CONV_DIM)[:, SUBLANES - (CONV_W - 1):])
        return o1, o2

    def outputs(self, y):
        return (y.reshape(self.shape), jnp.stack(self.new_gla), jnp.stack(self.new_gdn), jnp.stack(self.new_conv))


def _run(trunks, p, depth):
    n_all = sum(t.tok.n for t in trunks)
    ys = None
    for l in range(depth):
        i = l // 2
        last = l == depth - 1
        posts = []
        for t in trunks:
            o1, o2 = t.mix(l, p)
            wr = None if l % 2 == 0 else p["w_router"][i]
            posts.append(_post(t.tok, t.x, o1, o2, p["w_out"][l], t.mods[l], p["norm_ffn"][l], wr))
        if l % 2 == 0:
            for t, (x, h) in zip(trunks, posts):
                t.x = _ffn(t.tok, h, x, t.mods[l], p["w_ff_gate"][i], p["w_ff_up"][i], p["w_ff_down"][i])
        else:
            h_all = jnp.concatenate([h for _, h, _ in posts], axis=0)
            route = jnp.concatenate([r for _, _, r in posts], axis=0)
            y = _moe(h_all, route, p["w_exp_gate"][i], p["w_exp_up"][i], p["w_exp_down"][i])
            off = 0
            outs = []
            for t, (x, _, _) in zip(trunks, posts):
                fin = (t.mod_f, p["norm_final"]) if last else None
                outs.append(_combine(t.tok, x, y, route, t.mods[l], off, n_all, fin))
                off += t.tok.n
            if last:
                ys = outs
            else:
                for t, x in zip(trunks, outs):
                    t.x = x
    if ys is None:
        ys = [_final(t.tok, t.x, t.mod_f, p["norm_final"]) for t in trunks]
    return ys


def kernel(x_prompt, x_sample, c_prompt, c_sample, state_gla, state_gdn, state_conv, w_ada, b_ada, norm_mix, w_in, w_gla_gate, b_gla_gate, gla_norm, w_conv, a_log, dt_bias, gdn_norm, w_out, norm_ffn, w_ff_gate, w_ff_up, w_ff_down, w_router, w_exp_gate, w_exp_up, w_exp_down, w_ada_final, b_ada_final, norm_final):
    depth = w_in.shape[0]
    bp, bs = x_prompt.shape[0], x_sample.shape[0]
    dt = x_prompt.dtype

    wg_pad = jnp.pad(_pad_heads(w_gla_gate, GLA_HEADS, GLA_DK, GLA_KPAD), ((0, 0), (0, LANES - GLA_LOWRANK), (0, 0)))
    pad_lane = 1.0 - _pad_heads(jnp.ones_like(b_gla_gate), GLA_HEADS, GLA_DK, GLA_KPAD)
    bg_pad = _pad_heads(b_gla_gate, GLA_HEADS, GLA_DK, GLA_KPAD) + 30.0 * pad_lane
    wr = jnp.pad(w_router, ((0, 0), (0, 0), (0, LANES - N_EXPERTS)))
    wr_hi = wr.astype(BF16)
    wr_lo = (wr - wr_hi.astype(F32)).astype(BF16)
    p = {
        "norm_mix": norm_mix.reshape(depth, 1, D_MODEL),
        "w_in": jnp.stack([_layout_w_in(w_in[l]) for l in range(depth)]),
        "w_gate": wg_pad, "b_gate": bg_pad.reshape(depth, 1, -1),
        "gla_norm": gla_norm.reshape(depth, 1, GLA_DV),
        "w_conv": w_conv,
        "a_log": jnp.stack([_misc_row(a_log[l], MISC_A) for l in range(depth)]),
        "dt_bias": jnp.stack([_misc_row(dt_bias[l], MISC_A) for l in range(depth)]),
        "gdn_norm": gdn_norm.reshape(depth, 1, GDN_DV),
        "w_out": w_out.astype(BF16),
        "norm_ffn": norm_ffn.reshape(depth, 1, D_MODEL),
        "w_ff_gate": w_ff_gate.astype(BF16), "w_ff_up": w_ff_up.astype(BF16), "w_ff_down": w_ff_down.astype(BF16),
        "w_router": [(wr_hi[i], wr_lo[i]) for i in range(wr.shape[0])],
        "w_exp_gate": w_exp_gate, "w_exp_up": w_exp_up, "w_exp_down": w_exp_down,
        "norm_final": norm_final.reshape(1, D_MODEL),
    }

    c_all = jnp.concatenate([c_prompt, c_sample], axis=0)
    mods = [_ada(c_all, w_ada[l], b_ada[l]) for l in range(depth)]
    mod_f = _ada(c_all, w_ada_final, b_ada_final)
    z_gla = jnp.zeros((depth, bp, GLA_HEADS, GLA_DK, GLA_DV), dt)
    z_gdn = jnp.zeros((depth, bp, GDN_HEADS, GDN_DK, GDN_DV), dt)
    z_conv = jnp.zeros((depth, bp, CONV_W - 1, GDN_CONV_DIM), dt)
    prompt = _Trunk(x_prompt, [m[:bp] for m in mods], mod_f[:bp], z_gla, z_gdn, z_conv)
    sample = _Trunk(x_sample, [m[bp:] for m in mods], mod_f[bp:], state_gla, state_gdn, state_conv)
    y_p, y_s = _run([prompt, sample], p, depth)
    y_p, gla_p, gdn_p, conv_p = prompt.outputs(y_p)
    y_s, gla_s, gdn_s, conv_s = sample.outputs(y_s)
    return (y_p, y_s, gla_p, gdn_p, conv_p, gla_s, gdn_s, conv_s)
```

```python
import functools
import math

import jax
import jax.numpy as jnp
from jax import lax
from jax.experimental import pallas as pl
from jax.experimental.pallas import tpu as pltpu

F32 = jnp.float32
BF16 = jnp.bfloat16

D_MODEL = 1024
GLA_HEADS = 4
GLA_DK = 64
GLA_DV = 128
GLA_LOWRANK = 16
GLA_GATE_NORMALIZER = 16.0
GDN_HEADS = 4
GDN_DK = 128
GDN_DV = 128
CONV_W = 4
GDN_CONV_DIM = GDN_HEADS * (2 * GDN_DK + GDN_DV)
N_EXPERTS = 8
EPS = 1e-6

LANES = 128
SUBLANES = 8
BLOCK = 128
GLA_KPAD = 128
GLA_BAND = 60.0
VMEM_LIMIT = 56 * 1024 * 1024
PROJ_GROUP = 256

C_QKV = 0
C_Z = C_QKV + GDN_CONV_DIM
C_Q1 = C_Z + GDN_HEADS * GDN_DV
C_K1 = C_Q1 + GLA_HEADS * GLA_KPAD
C_V1 = C_K1 + GLA_HEADS * GLA_KPAD
C_G1 = C_V1 + GLA_HEADS * GLA_DV
C_MISC = C_G1 + GLA_HEADS * GLA_DV
PROJ_COLS = C_MISC + LANES
C_CUM = C_MISC
C_MISC_OUT = C_CUM + GLA_HEADS * GLA_KPAD
OUT_COLS = C_MISC_OUT + LANES
MISC_A = GLA_LOWRANK
MISC_B = GLA_LOWRANK + GDN_HEADS
TOP_K = 2
ROUTE_E = 0
ROUTE_W = TOP_K


def _dot(a, b, precision=None):
    return jnp.dot(a, b, preferred_element_type=F32, precision=precision)


def _dot_nt(a, b):
    return lax.dot_general(a, b, (((1,), (1,)), ((), ())), preferred_element_type=F32)


def _dot_tn(a, b):
    return lax.dot_general(a, b, (((0,), (0,)), ((), ())), preferred_element_type=F32)


def _mxu_pair(a, b):
    a, b = a.astype(BF16), b.astype(BF16)
    if a.shape[0] % (2 * SUBLANES) != 0 or b.shape[0] % (2 * SUBLANES) != 0:
        return a.astype(F32), b.astype(F32)
    return a, b


def _silu(x):
    return x * (1.0 / (1.0 + jnp.exp(-x)))


def _sigmoid(x):
    return 1.0 / (1.0 + jnp.exp(-x))


def _log_sigmoid(x):
    return jnp.minimum(x, 0.0) - jnp.log1p(jnp.exp(-jnp.abs(x)))


def _softplus(x):
    return jnp.maximum(x, 0.0) + jnp.log1p(jnp.exp(-jnp.abs(x)))


def _rms(x):
    return x * lax.rsqrt(jnp.mean(x * x, axis=-1, keepdims=True) + EPS)


def _per_token(mod_ref, tm):
    m = mod_ref[...]
    n_seq = m.shape[0]
    if n_seq == 1:
        return m
    shift = int(math.log2(tm // n_seq))
    assert n_seq << shift == tm
    row = lax.broadcasted_iota(jnp.int32, (tm, n_seq), 0)
    col = lax.broadcasted_iota(jnp.int32, (tm, n_seq), 1)
    return _dot(((row >> shift) == col).astype(F32), m, precision=lax.Precision.HIGHEST)


def _seg_cumsum(x, seg_len):
    pos = lax.broadcasted_iota(jnp.int32, x.shape, 0) & (seg_len - 1)
    s = 1
    while s < seg_len:
        x = x + jnp.where(pos >= s, pltpu.roll(x, s, axis=0), 0.0)
        s *= 2
    return x


def _seg_masks(seg_len):
    row = lax.broadcasted_iota(jnp.int32, (BLOCK, BLOCK), 0)
    col = lax.broadcasted_iota(jnp.int32, (BLOCK, BLOCK), 1)
    shift = int(math.log2(seg_len))
    same = (row >> shift) == (col >> shift)
    return same & (row >= col), same & (row > col), row == col


def _ada_body(c_ref, w_ref, b_ref, o_ref):
    cs = _silu(c_ref[...]).astype(BF16)
    o_ref[...] = _dot(cs, w_ref[...].astype(BF16)) + b_ref[...]


def _ada(c_all, w, b):
    rows = c_all.shape[0]
    nout = w.shape[1]
    tn = min(nout, 1024)
    return pl.pallas_call(
        _ada_body,
        grid=(nout // tn,),
        in_specs=[pl.BlockSpec((rows, D_MODEL), lambda j: (0, 0)),
                  pl.BlockSpec((D_MODEL, tn), lambda j: (0, j)),
                  pl.BlockSpec((1, tn), lambda j: (0, j))],
        out_specs=pl.BlockSpec((rows, tn), lambda j: (0, j)),
        out_shape=jax.ShapeDtypeStruct((rows, nout), F32),
        compiler_params=pltpu.CompilerParams(dimension_semantics=("arbitrary",)),
        name="ada_mod",
    )(c_all, w, b.reshape(1, nout))


class _Tokens:
    def __init__(self, n_seq, seq_len, tm):
        self.n_seq, self.seq_len, self.tm = n_seq, seq_len, tm
        self.n = n_seq * seq_len
        self.per_seq = seq_len % tm == 0
        self.tiles = self.n // tm

    def mod_operand(self, mod):
        if self.per_seq:
            return mod.reshape(self.n_seq, 1, mod.shape[1])
        return mod

    def mod_spec(self, col):
        if self.per_seq:
            tps = self.seq_len // self.tm
            return pl.BlockSpec((None, 1, D_MODEL), lambda i: (i // tps, 0, col))
        return pl.BlockSpec((self.tm // self.seq_len, D_MODEL), lambda i: (i, col))

    def row_spec(self, width, col=0):
        return pl.BlockSpec((self.tm, width), lambda i: (i, col))


def _const_spec(shape):
    nd = len(shape)
    return pl.BlockSpec(shape, lambda i: (0,) * nd, pipeline_mode=pl.Buffered(1))


def _inproj_body(x_ref, sh_ref, sc_ref, nw_ref, w_ref, wc_ref, wg_ref, bg_ref, cin_ref, o_ref, ctail_ref, prev_scr,
                 *, tiles_per_seq, gla_seg):
    tm = x_ref.shape[0]
    h = _rms(x_ref[...]) * nw_ref[...]
    h = (h * (1.0 + _per_token(sc_ref, tm)) + _per_token(sh_ref, tm)).astype(BF16)

    if tiles_per_seq:
        @pl.when(lax.rem(pl.program_id(0), tiles_per_seq) == 0)
        def _():
            prev_scr[...] = cin_ref[...]

    def conv_group(x, lo):
        cs = slice(lo, lo + x.shape[1])
        conv = x * wc_ref[CONV_W - 1:CONV_W, cs]
        for d in range(1, CONV_W):
            tap = pltpu.roll(x, d, axis=0)
            if tiles_per_seq:
                pos = lax.broadcasted_iota(jnp.int32, (SUBLANES, x.shape[1]), 0)
                head = jnp.where(pos >= d, tap[:SUBLANES], pltpu.roll(prev_scr[:, cs], d, axis=0))
                tap = jnp.concatenate([head, tap[SUBLANES:]], axis=0)
            else:
                pos = lax.broadcasted_iota(jnp.int32, x.shape, 0) & (SUBLANES - 1)
                tap = jnp.where(pos >= d, tap, pltpu.roll(cin_ref[:, cs], tm - SUBLANES + d, axis=0))
            conv = conv + tap * wc_ref[CONV_W - 1 - d:CONV_W - d, cs]
        o_ref[:, cs] = _silu(conv)
        if tiles_per_seq:
            prev_scr[:, cs] = x[tm - SUBLANES:]
            ctail_ref[:, cs] = x[tm - SUBLANES:]
        else:
            ctail_ref[:, cs] = x

    def store_group(y, lo):
        o_ref[:, lo:lo + y.shape[1]] = y

    def gate(misc, lo):
        pre = _dot(misc, wg_ref[...], precision=lax.Precision.HIGHEST) + bg_ref[...]
        o_ref[:, lo:lo + pre.shape[1]] = _seg_cumsum(_log_sigmoid(pre) * (1.0 / GLA_GATE_NORMALIZER), gla_seg)

    misc = _dot(h, w_ref[:, C_MISC:PROJ_COLS])
    o_ref[:, C_MISC_OUT:OUT_COLS] = misc
    conv_los = list(range(C_QKV, C_Z, PROJ_GROUP))
    store_los = list(range(C_Z, C_MISC, PROJ_GROUP))
    order = []
    while conv_los or store_los:
        order += conv_los[:1] + store_los[:1]
        conv_los, store_los = conv_los[1:], store_los[1:]
    post, post_args = gate, (misc, C_CUM)
    for lo in order:
        y = _dot(h, w_ref[:, lo:lo + PROJ_GROUP])
        post(*post_args)
        post, post_args = (conv_group if lo < C_Z else store_group), (y, lo)
    post(*post_args)


def _inproj(tok, x, mod, nw, w, wc, wg, bg, cin, layer):
    hk = GLA_HEADS * GLA_KPAD
    if tok.per_seq:
        tps = tok.seq_len // tok.tm
        cin_spec = pl.BlockSpec((None, SUBLANES, GDN_CONV_DIM), lambda i: (layer, i // tps, 0))
        ctail_spec = pl.BlockSpec((SUBLANES, GDN_CONV_DIM), lambda i: (i // tps, 0))
    else:
        assert tok.seq_len == SUBLANES
        tps = 0
        cin_spec = pl.BlockSpec((None, tok.tm, GDN_CONV_DIM), lambda i: (layer, i, 0))
        ctail_spec = pl.BlockSpec((tok.tm, GDN_CONV_DIM), lambda i: (i, 0))
    return pl.pallas_call(
        functools.partial(_inproj_body, tiles_per_seq=tps, gla_seg=min(tok.seq_len, BLOCK)),
        grid=(tok.tiles,),
        in_specs=[tok.row_spec(D_MODEL), tok.mod_spec(0), tok.mod_spec(1),
                  _const_spec((1, D_MODEL)), _const_spec((D_MODEL, PROJ_COLS)),
                  _const_spec((CONV_W, GDN_CONV_DIM)), _const_spec((LANES, hk)), _const_spec((1, hk)),
                  cin_spec],
        out_specs=[tok.row_spec(OUT_COLS), ctail_spec],
        out_shape=[jax.ShapeDtypeStruct((tok.n, OUT_COLS), F32),
                   jax.ShapeDtypeStruct((tok.n_seq * SUBLANES, GDN_CONV_DIM), F32)],
        scratch_shapes=[pltpu.VMEM((SUBLANES, GDN_CONV_DIM), F32)],
        compiler_params=pltpu.CompilerParams(dimension_semantics=("arbitrary",),
                                             vmem_limit_bytes=VMEM_LIMIT),
        name="in_proj",
    )(x, mod, mod, nw, w, wc, wg, bg, cin)


class _Mixer:
    def __init__(self, n_seq, seq_len):
        self.n_seq, self.seq_len = n_seq, seq_len
        self.carry = seq_len >= BLOCK
        if self.carry:
            self.nblk = 4
            self.rows = self.nblk * BLOCK
            self.grid = (n_seq, seq_len // self.rows)
            self.seqs = 1
        else:
            self.nblk = 1
            self.rows = BLOCK
            self.seqs = BLOCK // seq_len
            self.grid = (n_seq // self.seqs, 1)
        steps = self.grid[1]
        self.row_map = lambda b, t: b * steps + t

    def rows_spec(self, width, col):
        return pl.BlockSpec((self.rows, width), lambda b, t: (self.row_map(b, t), col))

    def state_spec(self, shape, layer=None):
        nd = len(shape)
        if layer is None:
            return pl.BlockSpec((self.seqs,) + shape, lambda b, t: (b,) + (0,) * nd)
        return pl.BlockSpec((None, self.seqs) + shape, lambda b, t: (layer, b) + (0,) * nd)


def _mixer_const(shape):
    nd = len(shape)
    return pl.BlockSpec(shape, lambda b, t: (0,) * nd)


def _gla_body(q_ref, k_ref, v_ref, g_ref, c_ref, nw_ref, s0_ref,
              o_ref, sout_ref, c_scr, a_scr, *, seg_len, carry, nblk):
    nseg = BLOCK // seg_len
    assert nseg == 1 or nseg % SUBLANES == 0
    incl, _, _ = _seg_masks(seg_len)
    heads = range(GLA_HEADS)
    if carry:
        @pl.when(pl.program_id(1) == 0)
        def _():
            sout_ref[...] = s0_ref[...]

    for blk in range(nblk):
        r0 = blk * BLOCK
        c = c_ref[r0:r0 + BLOCK, :]
        n_bands = (jnp.max(-c) * (1.0 / GLA_BAND)).astype(jnp.int32) + 1

        q_dec, k_in, v_in, c_in = [], [], [], []
        for h in heads:
            ks = slice(h * GLA_KPAD, (h + 1) * GLA_KPAD)
            ch = c[:, ks]
            c_scr[h] = ch
            qh = q_ref[r0:r0 + BLOCK, ks] * (GLA_DK ** -0.5)
            kh = k_ref[r0:r0 + BLOCK, ks]
            qd = qh * jnp.exp(ch)
            in0 = ch > -GLA_BAND
            k0 = jnp.where(in0, kh * jnp.exp(jnp.where(in0, -ch, 0.0)), 0.0).astype(BF16)
            a_scr[h] = _dot_nt(qd.astype(BF16), k0)
            q_dec.append(qd)
            k_in.append(kh)
            c_in.append(ch)
            v_in.append(v_ref[r0:r0 + BLOCK, h * GLA_DV:(h + 1) * GLA_DV])

        def band_body(r, carry_):
            lo = r.astype(F32) * GLA_BAND
            for h in heads:
                ks = slice(h * GLA_KPAD, (h + 1) * GLA_KPAD)
                ch = c_scr[h]
                qh = q_ref[r0:r0 + BLOCK, ks] * (GLA_DK ** -0.5)
                kh = k_ref[r0:r0 + BLOCK, ks]
                qb = (qh * jnp.exp(jnp.minimum(ch + lo, 0.0))).astype(BF16)
                inb = (ch <= -lo) & (ch > -(lo + GLA_BAND))
                kb = jnp.where(inb, kh * jnp.exp(jnp.where(inb, -ch - lo, 0.0)), 0.0).astype(BF16)
                a_scr[h] = a_scr[h] + _dot_nt(qb, kb)
            return carry_

        lax.fori_loop(1, n_bands, band_body, 0)

        decay_cols = []
        for h in heads:
            c_last = c_scr[h, pl.ds(seg_len - 1, nseg, stride=seg_len), :]
            if nseg == 1:
                c_last = jnp.broadcast_to(c_last, (SUBLANES, GLA_KPAD))
            c_last = jnp.concatenate([c_last, jnp.zeros((BLOCK - c_last.shape[0], GLA_KPAD), F32)], axis=0)
            decay_cols.append(jnp.exp(c_last.T[:GLA_DK, :]))
        o_intra = [_dot(jnp.where(incl, a_scr[h], 0.0).astype(BF16), v_in[h].astype(BF16)) for h in heads]
        o_inter = [[] for _ in heads]
        for sg in range(nseg):
            rs = slice(sg * seg_len, (sg + 1) * seg_len)
            seq = 0 if carry else blk * nseg + sg
            for h in heads:
                c_last = c_in[h][(sg + 1) * seg_len - 1:(sg + 1) * seg_len, :]
                st = sout_ref[seq, h] if carry else s0_ref[seq, h]
                o_inter[h].append(_dot(*_mxu_pair(q_dec[h][rs][:, :GLA_DK], st)))
                k_dec = (k_in[h][rs] * jnp.exp(c_last - c_in[h][rs]))[:, :GLA_DK]
                sout_ref[seq, h] = st * decay_cols[h][:, sg:sg + 1] + _dot_tn(*_mxu_pair(k_dec, v_in[h][rs]))
        for h in heads:
            vs = slice(h * GLA_DV, (h + 1) * GLA_DV)
            o_h = o_intra[h] + (o_inter[h][0] if nseg == 1 else jnp.concatenate(o_inter[h], axis=0))
            o_ref[r0:r0 + BLOCK, vs] = _rms(o_h) * nw_ref[...] * _silu(g_ref[r0:r0 + BLOCK, vs])


def _gla(mx, proj, nw, s0, layer):
    shape = (GLA_HEADS, GLA_DK, GLA_DV)
    body = functools.partial(_gla_body, seg_len=min(mx.seq_len, BLOCK), carry=mx.carry, nblk=mx.nblk)
    hk = GLA_HEADS * GLA_KPAD
    hv = GLA_HEADS * GLA_DV
    return pl.pallas_call(
        body,
        grid=mx.grid,
        in_specs=[mx.rows_spec(hk, C_Q1 // hk), mx.rows_spec(hk, C_K1 // hk),
                  mx.rows_spec(hv, C_V1 // hv), mx.rows_spec(hv, C_G1 // hv),
                  mx.rows_spec(hk, C_CUM // hk), _mixer_const((1, GLA_DV)),
                  mx.state_spec(shape, layer)],
        out_specs=[mx.rows_spec(hv, 0), mx.state_spec(shape)],
        out_shape=[jax.ShapeDtypeStruct((mx.n_seq * mx.seq_len, hv), F32),
                   jax.ShapeDtypeStruct((mx.n_seq,) + shape, F32)],
        scratch_shapes=[pltpu.VMEM((GLA_HEADS, BLOCK, GLA_KPAD), F32), pltpu.VMEM((GLA_HEADS, BLOCK, BLOCK), F32)],
        compiler_params=pltpu.CompilerParams(dimension_semantics=("arbitrary", "arbitrary"),
                                             vmem_limit_bytes=VMEM_LIMIT),
        name="gla_mixer",
    )(proj, proj, proj, proj, proj, nw, s0)


def _gdn_body(x_ref, z_ref, misc_ref, alog_ref, dtb_ref, nw_ref, s0_ref,
              o_ref, sout_ref, *, seg_len, carry, nblk):
    nseg = BLOCK // seg_len
    rows = nblk * BLOCK
    incl, strict, eye = _seg_masks(seg_len)
    eye_f = eye.astype(F32)
    row = lax.broadcasted_iota(jnp.int32, (BLOCK, BLOCK), 0)
    col = lax.broadcasted_iota(jnp.int32, (BLOCK, BLOCK), 1)
    off_masks = [((row >> (s + 1)) == (col >> (s + 1))) & ((row >> s) != (col >> s)) & (row > col)
                 for s in range(int(math.log2(seg_len)))]
    if carry:
        @pl.when(pl.program_id(1) == 0)
        def _():
            sout_ref[...] = s0_ref[...]

    chains = [(blk, h) for blk in range(nblk) for h in range(GDN_HEADS)]
    qs, ks, vs_, gcols, betas, egams, ms, aqks = {}, {}, {}, {}, {}, {}, {}, {}
    for blk in range(nblk):
        r0 = blk * BLOCK
        misc = misc_ref[r0:r0 + BLOCK, :]
        g = -jnp.exp(alog_ref[...]) * _softplus(misc + dtb_ref[...])
        beta_all = _sigmoid(misc)
        gam = _seg_cumsum(g, seg_len)
        gam_t = gam.T
        for h in range(GDN_HEADS):
            c = (blk, h)
            qh = x_ref[r0:r0 + BLOCK, h * GDN_DK:(h + 1) * GDN_DK]
            kh = x_ref[r0:r0 + BLOCK, (GDN_HEADS + h) * GDN_DK:(GDN_HEADS + h + 1) * GDN_DK]
            vs_[c] = x_ref[r0:r0 + BLOCK, 2 * GDN_HEADS * GDN_DK + h * GDN_DV:2 * GDN_HEADS * GDN_DK + (h + 1) * GDN_DV]
            qs[c] = qh * lax.rsqrt(jnp.sum(qh * qh, axis=-1, keepdims=True) + EPS) * (GDN_DK ** -0.5)
            ks[c] = kh * lax.rsqrt(jnp.sum(kh * kh, axis=-1, keepdims=True) + EPS)
            gcols[c] = gam[:, MISC_A + h:MISC_A + h + 1]
            grow = gam_t[MISC_A + h:MISC_A + h + 1, :]
            betas[c] = beta_all[:, MISC_B + h:MISC_B + h + 1]
            egams[c] = jnp.exp(gcols[c])
            kb = ks[c].astype(BF16)
            kq = _dot_nt(jnp.concatenate([kb, qs[c].astype(BF16)], axis=0), kb)
            lmask = jnp.exp(jnp.where(incl, gcols[c] - grow, -1e30))
            ms[c] = jnp.where(strict, betas[c] * kq[:BLOCK] * lmask, 0.0)
            aqks[c] = (kq[BLOCK:] * lmask).astype(BF16)

    invs = {c: eye_f - jnp.where(off_masks[0], ms[c], 0.0) for c in chains}
    for off in off_masks[1:]:
        tbs = {c: invs[c].astype(BF16) for c in chains}
        tms = {c: _dot(tbs[c], jnp.where(off, ms[c], 0.0).astype(BF16)).astype(BF16) for c in chains}
        invs = {c: invs[c] - _dot(tms[c], tbs[c]) for c in chains}
    us, ws, qdecs = {}, {}, {}
    for c in chains:
        rhs = jnp.concatenate([vs_[c] * betas[c], ks[c] * (betas[c] * egams[c])], axis=1).astype(BF16)
        uw = _dot(invs[c].astype(BF16), rhs)
        us[c], ws[c] = uw[:, :GDN_DV], uw[:, GDN_DV:]
        qdecs[c] = qs[c] * egams[c]

    v_new = {c: [] for c in chains}
    o_inter = {c: [] for c in chains}
    for blk in range(nblk):
        for sg in range(nseg):
            rs = slice(sg * seg_len, (sg + 1) * seg_len)
            seq = 0 if carry else blk * nseg + sg
            for h in range(GDN_HEADS):
                c = (blk, h)
                g_last = gcols[c][(sg + 1) * seg_len - 1:(sg + 1) * seg_len, :]
                st = sout_ref[seq, h] if carry else s0_ref[seq, h]
                k_dec = ks[c][rs] * jnp.exp(g_last - gcols[c][rs])
                if carry:
                    kwu = _dot_tn(*_mxu_pair(k_dec, jnp.concatenate([ws[c][rs], us[c][rs]], axis=1)))
                    sout_ref[seq, h] = (st * jnp.exp(g_last) + kwu[:, GDN_DK:]
                                        - _dot(*_mxu_pair(kwu[:, :GDN_DK], st)))
                wq = _dot(*_mxu_pair(jnp.concatenate([ws[c][rs], qdecs[c][rs]], axis=0), st))
                vn = us[c][rs] - wq[:seg_len]
                v_new[c].append(vn)
                o_inter[c].append(wq[seg_len:])
                if not carry:
                    sout_ref[seq, h] = st * jnp.exp(g_last) + _dot_tn(*_mxu_pair(k_dec, vn))

    for blk, h in chains:
        c = (blk, h)
        r0 = blk * BLOCK
        vn = v_new[c][0] if nseg == 1 else jnp.concatenate(v_new[c], axis=0)
        oi = o_inter[c][0] if nseg == 1 else jnp.concatenate(o_inter[c], axis=0)
        o_h = oi + _dot(aqks[c], vn.astype(BF16))
        cols = slice(h * GDN_DV, (h + 1) * GDN_DV)
        o_ref[r0:r0 + BLOCK, cols] = _rms(o_h) * nw_ref[...] * _silu(z_ref[r0:r0 + BLOCK, cols])


def _gdn(mx, proj, alog, dtb, nw, s0, layer):
    shape = (GDN_HEADS, GDN_DK, GDN_DV)
    body = functools.partial(_gdn_body, seg_len=min(mx.seq_len, BLOCK // 2), carry=mx.carry, nblk=mx.nblk)
    hv = GDN_HEADS * GDN_DV
    return pl.pallas_call(
        body,
        grid=mx.grid,
        in_specs=[mx.rows_spec(GDN_CONV_DIM, 0), mx.rows_spec(hv, C_Z // hv),
                  mx.rows_spec(LANES, C_MISC_OUT // LANES),
                  _mixer_const((1, LANES)), _mixer_const((1, LANES)), _mixer_const((1, GDN_DV)),
                  mx.state_spec(shape, layer)],
        out_specs=[mx.rows_spec(hv, 0), mx.state_spec(shape)],
        out_shape=[jax.ShapeDtypeStruct((mx.n_seq * mx.seq_len, hv), F32),
                   jax.ShapeDtypeStruct((mx.n_seq,) + shape, F32)],
        compiler_params=pltpu.CompilerParams(dimension_semantics=("arbitrary", "arbitrary"),
                                             vmem_limit_bytes=VMEM_LIMIT),
        name="gdn_mixer",
    )(proj, proj, proj, alog, dtb, nw, s0)


def _split_bf16(x):
    hi = x.astype(BF16)
    return hi, (x - hi.astype(F32)).astype(BF16)


def _post_body(x_ref, o1_ref, o2_ref, wo_ref, gt_ref, sh_ref, sc_ref, nw_ref, *rest, router):
    if router:
        wr_hi_ref, wr_lo_ref, xo_ref, h_ref, comb_ref = rest
    else:
        xo_ref, h_ref = rest
    half = GLA_HEADS * GLA_DV
    mix = _dot(o1_ref[...].astype(BF16), wo_ref[:half, :]) + _dot(o2_ref[...].astype(BF16), wo_ref[half:, :])
    tm = x_ref.shape[0]
    x = x_ref[...] + _per_token(gt_ref, tm) * mix
    xo_ref[...] = x
    h = _rms(x) * nw_ref[...]
    h = h * (1.0 + _per_token(sc_ref, tm)) + _per_token(sh_ref, tm)
    h_ref[...] = h.astype(h_ref.dtype)
    if router:
        h_hi, h_lo = _split_bf16(h)
        logits = _dot(h_hi, wr_hi_ref[...]) + (_dot(h_hi, wr_lo_ref[...]) + _dot(h_lo, wr_hi_ref[...]))
        lane = lax.broadcasted_iota(jnp.int32, logits.shape, 1)
        lg = jnp.where(lane < N_EXPERTS, logits, -jnp.inf)
        m1 = jnp.max(lg, axis=-1, keepdims=True)
        i1 = jnp.min(jnp.where(lg == m1, lane, LANES), axis=-1, keepdims=True)
        lg2 = jnp.where(lane == i1, -jnp.inf, lg)
        m2 = jnp.max(lg2, axis=-1, keepdims=True)
        i2 = jnp.min(jnp.where(lg2 == m2, lane, LANES), axis=-1, keepdims=True)
        e2 = jnp.exp(m2 - m1)
        w1 = 1.0 / (1.0 + e2)
        comb_ref[...] = (jnp.where(lane == ROUTE_E, i1.astype(F32), 0.0)
                         + jnp.where(lane == ROUTE_E + 1, i2.astype(F32), 0.0)
                         + jnp.where(lane == ROUTE_W, w1, 0.0)
                         + jnp.where(lane == ROUTE_W + 1, e2 * w1, 0.0))


def _post(tok, x, o1, o2, wo, mod, nw, wr=None):
    router = wr is not None
    half = GLA_HEADS * GLA_DV
    in_specs = [tok.row_spec(D_MODEL), tok.row_spec(half), tok.row_spec(half),
                _const_spec((D_MODEL, D_MODEL)), tok.mod_spec(2), tok.mod_spec(3), tok.mod_spec(4),
                _const_spec((1, D_MODEL))]
    args = [x, o1, o2, wo, mod, mod, mod, nw]
    out_specs = [tok.row_spec(D_MODEL), tok.row_spec(D_MODEL)]
    out_shape = [jax.ShapeDtypeStruct((tok.n, D_MODEL), F32),
                 jax.ShapeDtypeStruct((tok.n, D_MODEL), F32 if router else BF16)]
    if router:
        in_specs += [_const_spec((D_MODEL, LANES)), _const_spec((D_MODEL, LANES))]
        args += list(wr)
        out_specs.append(tok.row_spec(LANES))
        out_shape.append(jax.ShapeDtypeStruct((tok.n, LANES), F32))
    return pl.pallas_call(
        functools.partial(_post_body, router=router),
        grid=(tok.tiles,),
        in_specs=in_specs, out_specs=out_specs, out_shape=out_shape,
        compiler_params=pltpu.CompilerParams(dimension_semantics=("arbitrary",),
                                             vmem_limit_bytes=VMEM_LIMIT),
        name="post_mixer",
    )(*args)


def _ffn_body(h_ref, x_ref, gt_ref, wg_ref, wu_ref, wd_ref, o_ref, *, chunk):
    h = h_ref[...]
    d_ff = wg_ref.shape[1]
    acc = None
    for lo in range(0, d_ff, chunk):
        g = _dot(h, wg_ref[:, lo:lo + chunk])
        u = _dot(h, wu_ref[:, lo:lo + chunk])
        part = _dot((_silu(g) * u).astype(BF16), wd_ref[lo:lo + chunk, :])
        acc = part if acc is None else acc + part
    o_ref[...] = x_ref[...] + _per_token(gt_ref, x_ref.shape[0]) * acc


def _ffn(tok, h, x, mod, wg, wu, wd):
    d_ff = wg.shape[1]
    return pl.pallas_call(
        functools.partial(_ffn_body, chunk=d_ff // 2),
        grid=(tok.tiles,),
        in_specs=[tok.row_spec(D_MODEL), tok.row_spec(D_MODEL), tok.mod_spec(5),
                  _const_spec((D_MODEL, d_ff)), _const_spec((D_MODEL, d_ff)), _const_spec((d_ff, D_MODEL))],
        out_specs=tok.row_spec(D_MODEL),
        out_shape=jax.ShapeDtypeStruct((tok.n, D_MODEL), F32),
        compiler_params=pltpu.CompilerParams(dimension_semantics=("arbitrary",),
                                             vmem_limit_bytes=VMEM_LIMIT),
        name="ffn_dense",
    )(h, x, mod, wg, wu, wd)


MOE_TM = 512
MOE_CHUNK = 512


def _route_plan(route, tm):
    n = route.shape[0]
    na = n * TOP_K
    flat_e = route[:, ROUTE_E:ROUTE_E + TOP_K].astype(jnp.int32).T.reshape(na)
    onehot = (flat_e[:, None] == jnp.arange(N_EXPERTS, dtype=jnp.int32)[None, :]).astype(jnp.int32)
    cum = jnp.cumsum(onehot, axis=0)
    rank = jnp.sum(onehot * cum, axis=1) - 1
    counts = cum[-1]
    padded = ((counts + tm - 1) // tm) * tm
    ends = jnp.cumsum(padded)
    starts = ends - padded
    pos = starts[flat_e] + rank
    tiles = -(-na // tm) + N_EXPERTS
    dst = jnp.zeros((tiles * tm,), jnp.int32).at[pos].set(jnp.arange(na, dtype=jnp.int32))
    dst = dst.reshape(tiles, tm)
    tile_start = jnp.arange(tiles, dtype=jnp.int32) * tm
    tile_expert = jnp.sum((ends[None, :] <= tile_start[:, None]).astype(jnp.int32), axis=1)
    tile_expert = jnp.minimum(tile_expert, N_EXPERTS - 1)
    n_valid = (ends[-1] // tm).astype(jnp.int32).reshape(1)
    n_rows = jnp.clip((starts + counts)[tile_expert] - tile_start, 0, tm).astype(jnp.int32)
    real = jnp.arange(tm, dtype=jnp.int32)[None, :] < n_rows[:, None]
    src = jnp.where(real, dst % n, 0)
    return tile_expert, n_valid, n_rows, src.reshape(tiles, 1, tm), dst.reshape(tiles, 1, tm)


def _moe_body(te_ref, nv_ref, nr_ref, src_ref, srcn_ref, dstp_ref, dst_ref, h_hbm, wg_hbm, wu_hbm, wd_hbm, y_hbm,
              hbuf, ybuf, wg_res, wu_res, wd_res, stage, stage_d, gsem, ssem, wsem, *, tm, chunk):
    i = pl.program_id(0)
    last = pl.num_programs(0) - 1
    n_valid = nv_ref[0]
    slot = lax.rem(i, 2)
    other = 1 - slot
    valid = i < n_valid
    next_valid = i + 1 < n_valid
    prev_valid = (i >= 1) & (i - 1 < n_valid)
    n_prev = jnp.where(prev_valid, nr_ref[jnp.maximum(i - 1, 0)], 0)
    expert = te_ref[i]
    new_expert = (i == 0) | (expert != te_ref[jnp.maximum(i - 1, 0)])
    d_ff = wg_res.shape[1]
    n_chunks = d_ff // chunk

    def gather_row(idx_ref, s, r):
        return pltpu.make_async_copy(h_hbm.at[pl.ds(idx_ref[0, r], 1)], hbuf.at[s, pl.ds(r, 1)], gsem.at[s])

    def scatter_row(idx_ref, s, r):
        return pltpu.make_async_copy(ybuf.at[s, pl.ds(r, 1)], y_hbm.at[pl.ds(idx_ref[0, r], 1)], ssem.at[s])

    def weight_copies(c):
        ws, cols = c % 2, pl.ds(c * chunk, chunk)
        return (pltpu.make_async_copy(wg_hbm.at[expert, :, cols], stage.at[ws, 0], wsem.at[ws]),
                pltpu.make_async_copy(wu_hbm.at[expert, :, cols], stage.at[ws, 1], wsem.at[ws]),
                pltpu.make_async_copy(wd_hbm.at[expert, cols, :], stage_d.at[ws], wsem.at[ws]))

    def gather_wait(s):
        pltpu.make_async_copy(h_hbm.at[pl.ds(0, tm)], hbuf.at[s], gsem.at[s]).wait()

    def scatter_wait(s, n):
        n_tiled = pl.multiple_of(lax.shift_left(lax.shift_right_logical(n, 3), 3), SUBLANES)

        @pl.when(n_tiled > 0)
        def _():
            pltpu.make_async_copy(ybuf.at[s, pl.ds(0, n_tiled)], y_hbm.at[pl.ds(0, n_tiled)], ssem.at[s]).wait()

        def body(r, c):
            scatter_row(dst_ref, s, 0).wait()
            return c
        lax.fori_loop(0, n - n_tiled, body, 0)

    def issue_rows(lo, hi):
        for r in range(lo, hi):
            @pl.when(next_valid)
            def _():
                gather_row(srcn_ref, other, r).start()

            @pl.when(r < n_prev)
            def _():
                scatter_row(dstp_ref, other, r).start()

    def compute_tile(load_weights):
        gather_wait(slot)
        x = hbuf[slot].astype(BF16)
        rows_per_chunk = -(-tm // n_chunks)
        if load_weights:
            for c in range(min(2, n_chunks)):
                for cp in weight_copies(c):
                    cp.start()
        acc = None
        for c in range(n_chunks):
            lo = c * chunk
            if load_weights:
                for cp in weight_copies(c):
                    cp.wait()
                ws = c % 2
                wg_res[:, lo:lo + chunk] = stage[ws, 0].astype(BF16)
                wu_res[:, lo:lo + chunk] = stage[ws, 1].astype(BF16)
                wd_res[lo:lo + chunk, :] = stage_d[ws].astype(BF16)
                if c + 2 < n_chunks:
                    for cp in weight_copies(c + 2):
                        cp.start()
            g = _dot(x, wg_res[:, lo:lo + chunk])
            u = _dot(x, wu_res[:, lo:lo + chunk])
            issue_rows(min(c * rows_per_chunk, tm), min((c + 1) * rows_per_chunk, tm))
            part = _dot((_silu(g) * u).astype(BF16), wd_res[lo:lo + chunk, :])
            acc = part if acc is None else acc + part
        ybuf[slot] = acc

    @pl.when(i == 0)
    def _():
        def body(r, c):
            gather_row(src_ref, 0, r).start()
            return c
        lax.fori_loop(0, tm, body, 0, unroll=8)

    @pl.when((i >= 2) & (i - 2 < n_valid))
    def _():
        scatter_wait(slot, nr_ref[jnp.maximum(i - 2, 0)])

    @pl.when(valid & new_expert)
    def _():
        compute_tile(True)

    @pl.when(valid & jnp.logical_not(new_expert))
    def _():
        compute_tile(False)

    @pl.when(jnp.logical_not(valid) & prev_valid)
    def _():
        def body(r, c):
            scatter_row(dstp_ref, other, r).start()
            return c
        lax.fori_loop(0, n_prev, body, 0)

    @pl.when(i == last)
    def _():
        @pl.when(valid)
        def _():
            def body(r, c):
                scatter_row(dst_ref, slot, r).start()
                return c
            lax.fori_loop(0, nr_ref[i], body, 0)

        @pl.when(prev_valid)
        def _():
            scatter_wait(other, n_prev)

        @pl.when(valid)
        def _():
            scatter_wait(slot, nr_ref[i])


def _moe(h, route, wg, wu, wd):
    n = h.shape[0]
    tm = MOE_TM
    d_ff = wg.shape[2]
    assert d_ff % MOE_CHUNK == 0
    tile_expert, n_valid, n_rows, src, dst = _route_plan(route, tm)
    tiles = src.shape[0]
    idx_spec = lambda f: pl.BlockSpec((None, 1, tm), f, memory_space=pltpu.SMEM)
    any_spec = pl.BlockSpec(memory_space=pl.ANY)
    grid_spec = pltpu.PrefetchScalarGridSpec(
        num_scalar_prefetch=3,
        grid=(tiles,),
        in_specs=[idx_spec(lambda i, te, nv, nr: (i, 0, 0)),
                  idx_spec(lambda i, te, nv, nr: (jnp.minimum(i + 1, tiles - 1), 0, 0)),
                  idx_spec(lambda i, te, nv, nr: (jnp.maximum(i - 1, 0), 0, 0)),
                  idx_spec(lambda i, te, nv, nr: (i, 0, 0)),
                  any_spec, any_spec, any_spec, any_spec],
        out_specs=any_spec,
        scratch_shapes=[pltpu.VMEM((2, tm, D_MODEL), F32), pltpu.VMEM((2, tm, D_MODEL), F32),
                        pltpu.VMEM((D_MODEL, d_ff), BF16), pltpu.VMEM((D_MODEL, d_ff), BF16),
                        pltpu.VMEM((d_ff, D_MODEL), BF16),
                        pltpu.VMEM((2, 2, D_MODEL, MOE_CHUNK), F32), pltpu.VMEM((2, MOE_CHUNK, D_MODEL), F32),
                        pltpu.SemaphoreType.DMA((2,)), pltpu.SemaphoreType.DMA((2,)),
                        pltpu.SemaphoreType.DMA((2,))],
    )
    return pl.pallas_call(
        functools.partial(_moe_body, tm=tm, chunk=MOE_CHUNK),
        grid_spec=grid_spec,
        out_shape=jax.ShapeDtypeStruct((n * TOP_K, D_MODEL), F32),
        compiler_params=pltpu.CompilerParams(dimension_semantics=("arbitrary",),
                                             vmem_limit_bytes=VMEM_LIMIT),
        name="moe_routed",
    )(tile_expert, n_valid, n_rows, src, src, dst, dst, h, wg, wu, wd)


def _combine_body(x_ref, y0_ref, y1_ref, rt_ref, gt_ref, *rest, final):
    rt = rt_ref[...]
    f = rt[:, ROUTE_W:ROUTE_W + 1] * y0_ref[...] + rt[:, ROUTE_W + 1:ROUTE_W + 2] * y1_ref[...]
    tm = x_ref.shape[0]
    x = x_ref[...] + _per_token(gt_ref, tm) * f
    if final:
        sh_ref, sc_ref, nw_ref, o_ref = rest
        x = _rms(x) * nw_ref[...]
        x = x * (1.0 + _per_token(sc_ref, tm)) + _per_token(sh_ref, tm)
    else:
        (o_ref,) = rest
    o_ref[...] = x


def _combine(tok, x, y, route, mod, tok_offset, n_all, final=None):
    b0 = tok_offset // tok.tm
    b1 = (n_all + tok_offset) // tok.tm
    in_specs = [tok.row_spec(D_MODEL),
                pl.BlockSpec((tok.tm, D_MODEL), lambda i: (b0 + i, 0)),
                pl.BlockSpec((tok.tm, D_MODEL), lambda i: (b1 + i, 0)),
                pl.BlockSpec((tok.tm, LANES), lambda i: (b0 + i, 0)),
                tok.mod_spec(5)]
    args = [x, y, y, route, mod]
    if final is not None:
        mod_f, nw = final
        in_specs += [tok.mod_spec(0), tok.mod_spec(1), _const_spec((1, D_MODEL))]
        args += [mod_f, mod_f, nw]
    return pl.pallas_call(
        functools.partial(_combine_body, final=final is not None),
        grid=(tok.tiles,),
        in_specs=in_specs,
        out_specs=tok.row_spec(D_MODEL),
        out_shape=jax.ShapeDtypeStruct((tok.n, D_MODEL), F32),
        compiler_params=pltpu.CompilerParams(dimension_semantics=("arbitrary",),
                                             vmem_limit_bytes=VMEM_LIMIT),
        name="moe_combine",
    )(*args)


def _final_body(x_ref, sh_ref, sc_ref, nw_ref, o_ref):
    tm = x_ref.shape[0]
    h = _rms(x_ref[...]) * nw_ref[...]
    o_ref[...] = h * (1.0 + _per_token(sc_ref, tm)) + _per_token(sh_ref, tm)


def _final(tok, x, mod, nw):
    return pl.pallas_call(
        _final_body,
        grid=(tok.tiles,),
        in_specs=[tok.row_spec(D_MODEL), tok.mod_spec(0), tok.mod_spec(1), _const_spec((1, D_MODEL))],
        out_specs=tok.row_spec(D_MODEL),
        out_shape=jax.ShapeDtypeStruct((tok.n, D_MODEL), F32),
        compiler_params=pltpu.CompilerParams(dimension_semantics=("arbitrary",)),
        name="final_norm",
    )(x, mod, mod, nw)


def _pad_heads(w, heads, dk, kpad):
    lead = w.shape[:-1]
    w = w.reshape(lead + (heads, dk))
    w = jnp.pad(w, [(0, 0)] * len(lead) + [(0, 0), (0, kpad - dk)])
    return w.reshape(lead + (heads * kpad,))


def _layout_w_in(w):
    hk, hv = GLA_HEADS * GLA_DK, GLA_HEADS * GLA_DV
    o = 0
    q1 = w[:, o:o + hk]; o += hk
    k1 = w[:, o:o + hk]; o += hk
    v1 = w[:, o:o + hv]; o += hv
    lr = w[:, o:o + GLA_LOWRANK]; o += GLA_LOWRANK
    g1 = w[:, o:o + hv]; o += hv
    qkv = w[:, o:o + GDN_CONV_DIM]; o += GDN_CONV_DIM
    a = w[:, o:o + GDN_HEADS]; o += GDN_HEADS
    b = w[:, o:o + GDN_HEADS]; o += GDN_HEADS
    z = w[:, o:o + GDN_HEADS * GDN_DV]
    misc = jnp.concatenate([lr, a, b], axis=1)
    misc = jnp.pad(misc, ((0, 0), (0, LANES - misc.shape[1])))
    out = jnp.concatenate([qkv, z, _pad_heads(q1, GLA_HEADS, GLA_DK, GLA_KPAD),
                           _pad_heads(k1, GLA_HEADS, GLA_DK, GLA_KPAD), v1, g1, misc], axis=1)
    return out.astype(BF16)


def _misc_row(v, lane0):
    return jnp.pad(v.astype(F32), (lane0, LANES - lane0 - v.shape[0])).reshape(1, LANES)


class _Trunk:
    def __init__(self, x, mods, mod_f, s_gla, s_gdn, s_conv):
        n_seq, seq_len, _ = x.shape
        self.shape = x.shape
        self.tok = _Tokens(n_seq, seq_len, MOE_TM)
        self.mx = _Mixer(n_seq, seq_len)
        self.x = x.reshape(self.tok.n, D_MODEL)
        self.mods = [self.tok.mod_operand(m) for m in mods]
        self.mod_f = self.tok.mod_operand(mod_f)
        cin = jnp.pad(s_conv, ((0, 0), (0, 0), (SUBLANES - (CONV_W - 1), 0), (0, 0)))
        self.s_gla, self.s_gdn, self.cin = s_gla, s_gdn, cin.reshape(s_conv.shape[0], n_seq * SUBLANES, GDN_CONV_DIM)
        self.new_gla, self.new_gdn, self.new_conv = [], [], []

    def mix(self, l, p):
        n_seq = self.shape[0]
        proj, ctail = _inproj(self.tok, self.x, self.mods[l], p["norm_mix"][l], p["w_in"][l], p["w_conv"][l],
                              p["w_gate"][l], p["b_gate"][l], self.cin, l)
        o1, sa = _gla(self.mx, proj, p["gla_norm"][l], self.s_gla, l)
        o2, sb = _gdn(self.mx, proj, p["a_log"][l], p["dt_bias"][l], p["gdn_norm"][l], self.s_gdn, l)
        self.new_gla.append(sa)
        self.new_gdn.append(sb)
        self.new_conv.append(ctail.reshape(n_seq, SUBLANES, GDN_CONV_DIM)[:, SUBLANES - (CONV_W - 1):])
        return o1, o2

    def outputs(self, y):
        return (y.reshape(self.shape), jnp.stack(self.new_gla), jnp.stack(self.new_gdn), jnp.stack(self.new_conv))


def _run(trunks, p, depth):
    n_all = sum(t.tok.n for t in trunks)
    ys = None
    for l in range(depth):
        i = l // 2
        last = l == depth - 1
        posts = []
        for t in trunks:
            o1, o2 = t.mix(l, p)
            wr = None if l % 2 == 0 else p["w_router"][i]
            posts.append(_post(t.tok, t.x, o1, o2, p["w_out"][l], t.mods[l], p["norm_ffn"][l], wr))
        if l % 2 == 0:
            for t, (x, h) in zip(trunks, posts):
                t.x = _ffn(t.tok, h, x, t.mods[l], p["w_ff_gate"][i], p["w_ff_up"][i], p["w_ff_down"][i])
        else:
            h_all = jnp.concatenate([h for _, h, _ in posts], axis=0)
            route = jnp.concatenate([r for _, _, r in posts], axis=0)
            y = _moe(h_all, route, p["w_exp_gate"][i], p["w_exp_up"][i], p["w_exp_down"][i])
            off = 0
            outs = []
            for t, (x, _, _) in zip(trunks, posts):
                fin = (t.mod_f, p["norm_final"]) if last else None
                outs.append(_combine(t.tok, x, y, route, t.mods[l], off, n_all, fin))
                off += t.tok.n
            if last:
                ys = outs
            else:
                for t, x in zip(trunks, outs):
                    t.x = x
    if ys is None:
        ys = [_final(t.tok, t.x, t.mod_f, p["norm_final"]) for t in trunks]
    return ys


def kernel(x_prompt, x_sample, c_prompt, c_sample, state_gla, state_gdn, state_conv, w_ada, b_ada, norm_mix, w_in, w_gla_gate, b_gla_gate, gla_norm, w_conv, a_log, dt_bias, gdn_norm, w_out, norm_ffn, w_ff_gate, w_ff_up, w_ff_down, w_router, w_exp_gate, w_exp_up, w_exp_down, w_ada_final, b_ada_final, norm_final):
    depth = w_in.shape[0]
    bp, bs = x_prompt.shape[0], x_sample.shape[0]
    dt = x_prompt.dtype

    wg_pad = jnp.pad(_pad_heads(w_gla_gate, GLA_HEADS, GLA_DK, GLA_KPAD), ((0, 0), (0, LANES - GLA_LOWRANK), (0, 0)))
    pad_lane = 1.0 - _pad_heads(jnp.ones_like(b_gla_gate), GLA_HEADS, GLA_DK, GLA_KPAD)
    bg_pad = _pad_heads(b_gla_gate, GLA_HEADS, GLA_DK, GLA_KPAD) + 30.0 * pad_lane
    wr = jnp.pad(w_router, ((0, 0), (0, 0), (0, LANES - N_EXPERTS)))
    wr_hi = wr.astype(BF16)
    wr_lo = (wr - wr_hi.astype(F32)).astype(BF16)
    p = {
        "norm_mix": norm_mix.reshape(depth, 1, D_MODEL),
        "w_in": jnp.stack([_layout_w_in(w_in[l]) for l in range(depth)]),
        "w_gate": wg_pad, "b_gate": bg_pad.reshape(depth, 1, -1),
        "gla_norm": gla_norm.reshape(depth, 1, GLA_DV),
        "w_conv": w_conv,
        "a_log": jnp.stack([_misc_row(a_log[l], MISC_A) for l in range(depth)]),
        "dt_bias": jnp.stack([_misc_row(dt_bias[l], MISC_A) for l in range(depth)]),
        "gdn_norm": gdn_norm.reshape(depth, 1, GDN_DV),
        "w_out": w_out.astype(BF16),
        "norm_ffn": norm_ffn.reshape(depth, 1, D_MODEL),
        "w_ff_gate": w_ff_gate.astype(BF16), "w_ff_up": w_ff_up.astype(BF16), "w_ff_down": w_ff_down.astype(BF16),
        "w_router": [(wr_hi[i], wr_lo[i]) for i in range(wr.shape[0])],
        "w_exp_gate": w_exp_gate, "w_exp_up": w_exp_up, "w_exp_down": w_exp_down,
        "norm_final": norm_final.reshape(1, D_MODEL),
    }

    c_all = jnp.concatenate([c_prompt, c_sample], axis=0)
    mods = [_ada(c_all, w_ada[l], b_ada[l]) for l in range(depth)]
    mod_f = _ada(c_all, w_ada_final, b_ada_final)
    z_gla = jnp.zeros((depth, bp, GLA_HEADS, GLA_DK, GLA_DV), dt)
    z_gdn = jnp.zeros((depth, bp, GDN_HEADS, GDN_DK, GDN_DV), dt)
    z_conv = jnp.zeros((depth, bp, CONV_W - 1, GDN_CONV_DIM), dt)
    prompt = _Trunk(x_prompt, [m[:bp] for m in mods], mod_f[:bp], z_gla, z_gdn, z_conv)
    sample = _Trunk(x_sample, [m[bp:] for m in mods], mod_f[bp:], state_gla, state_gdn, state_conv)
    y_p, y_s = _run([prompt, sample], p, depth)
    y_p, gla_p, gdn_p, conv_p = prompt.outputs(y_p)
    y_s, gla_s, gdn_s, conv_s = sample.outputs(y_s)
    return (y_p, y_s, gla_p, gdn_p, conv_p, gla_s, gdn_s, conv_s)
```
